```python
import jax, jax.numpy as jnp
from jax import lax
import numpy as np

D_MODEL = 2048
BATCH = 16
SEQ = 2048
DEPTH = 4

D_MIX = D_MODEL
N_MIXERS = 4
GROUP_W = D_MIX // N_MIXERS
ATT_HEADS = 4
HEAD_DIM = GROUP_W // ATT_HEADS
Q_BLOCK = 128
CONF_K = 31
SC_K = 3
POOL_WINDOWS = (2, 4, 8, 16)
POOL_GROUP = GROUP_W // len(POOL_WINDOWS)
D_FF = 4 * D_MODEL
EPS = 1e-6
N_ATT_COLS = 3 * GROUP_W + ATT_HEADS
N_CONF_COLS = 2 * GROUP_W
N_POOL_COLS = GROUP_W
N_SC_COLS = 3 * GROUP_W
D_IN = N_ATT_COLS + N_CONF_COLS + N_POOL_COLS + N_SC_COLS

kernel_name = "hybrid_parallel_heads_fox_conformer_pool_shortconv"


def rmsnorm(x, g):
    xf = x.astype(jnp.float32)
    y = xf * lax.rsqrt(jnp.mean(xf * xf, axis=-1, keepdims=True) + EPS)
    return (y * g.astype(jnp.float32)).astype(x.dtype)


def layernorm(x, g, b):
    xf = x.astype(jnp.float32)
    mu = jnp.mean(xf, axis=-1, keepdims=True)
    var = jnp.mean(jnp.square(xf - mu), axis=-1, keepdims=True)
    y = (xf - mu) * lax.rsqrt(var + EPS)
    return (y * g.astype(jnp.float32) + b.astype(jnp.float32)).astype(x.dtype)


def causal_depthwise_conv(x, w):
    k, c = w.shape
    return lax.conv_general_dilated(
        x, w[:, None, :].astype(x.dtype), window_strides=(1,), padding=[(k - 1, 0)],
        dimension_numbers=("NWC", "WIO", "NWC"), feature_group_count=c)


def forgetting_attention(q, k, v, log_f):
    s_len = q.shape[1]
    cum = jnp.transpose(jnp.cumsum(log_f, axis=1), (0, 2, 1))
    scale = 1.0 / np.sqrt(HEAD_DIM)
    outs = []
    for i in range(s_len // Q_BLOCK):
        q0, q1 = i * Q_BLOCK, (i + 1) * Q_BLOCK
        qb = q[:, q0:q1]
        kb, vb = k[:, :q1], v[:, :q1]
        scores = jnp.einsum("bqhd,bkhd->bhqk", qb, kb).astype(jnp.float32) * scale
        decay = cum[:, :, q0:q1, None] - cum[:, :, None, :q1]
        q_pos = jnp.arange(q0, q1)[:, None]
        k_pos = jnp.arange(q1)[None, :]
        logits = jnp.where(k_pos <= q_pos, scores + decay, -jnp.inf)
        p = jax.nn.softmax(logits, axis=-1)
        outs.append(jnp.einsum("bhqk,bkhd->bqhd", p.astype(vb.dtype), vb))
    return jnp.concatenate(outs, axis=1)


def multiscale_pool(xp, pool_w, pool_scale):
    b, s, _ = xp.shape
    xg = xp.reshape(b, s, len(POOL_WINDOWS), POOL_GROUP)
    pos = jnp.arange(s)
    pooled = []
    for g, w in enumerate(POOL_WINDOWS):
        xf = xg[:, :, g].astype(jnp.float32)
        cs = jnp.cumsum(xf, axis=1)
        prev = jnp.pad(cs, ((0, 0), (w, 0), (0, 0)))[:, :s]
        count = jnp.minimum(pos + 1, w).astype(jnp.float32)[None, :, None]
        pooled.append((cs - prev) / count - xf)
    pooled = jnp.stack(pooled, axis=2).astype(xp.dtype)
    y = jnp.einsum("bsgc,gcd->bsgd", pooled, pool_w).reshape(b, s, GROUP_W)
    return y * pool_scale


def setup_inputs(seed: int = 0) -> dict:
    key = jax.random.key(seed)
    ks = jax.random.split(key, 20)
    f32 = jnp.float32
    nrm = lambda k, shape, s: jax.random.normal(k, shape, f32) * s
    gain = lambda k, shape: 1.0 + 0.05 * jax.random.normal(k, shape, f32)
    return {
        "x": jax.random.normal(ks[0], (BATCH, SEQ, D_MODEL), f32),
        "mix_norm_pre": gain(ks[1], (DEPTH, D_MODEL)),
        "w_in": nrm(ks[2], (DEPTH, D_MODEL, D_IN), D_MODEL ** -0.5),
        "b_forget": 3.0 + 0.5 * jax.random.normal(ks[3], (DEPTH, ATT_HEADS), f32),
        "conf_dw": nrm(ks[4], (DEPTH, CONF_K, GROUP_W), CONF_K ** -0.5),
        "conf_ln_g": gain(ks[5], (DEPTH, GROUP_W)),
        "conf_ln_b": nrm(ks[6], (DEPTH, GROUP_W), 0.02),
        "pool_w": nrm(ks[7], (DEPTH, len(POOL_WINDOWS), POOL_GROUP, POOL_GROUP), POOL_GROUP ** -0.5),
        "pool_scale": gain(ks[8], (DEPTH, GROUP_W)),
        "sc_dw": nrm(ks[9], (DEPTH, SC_K, GROUP_W), SC_K ** -0.5),
        "w_out": nrm(ks[10], (DEPTH, D_MIX, D_MODEL), D_MIX ** -0.5),
        "mix_norm_post": gain(ks[11], (DEPTH, D_MODEL)),
        "mlp_norm_pre": gain(ks[12], (DEPTH, D_MODEL)),
        "w_mlp1": nrm(ks[13], (DEPTH, D_MODEL, D_FF), D_MODEL ** -0.5),
        "w_mlp2": nrm(ks[14], (DEPTH, D_FF, D_MODEL), D_FF ** -0.5),
        "mlp_norm_post": gain(ks[15], (DEPTH, D_MODEL)),
    }


def reference(x, mix_norm_pre, w_in, b_forget, conf_dw, conf_ln_g, conf_ln_b,
              pool_w, pool_scale, sc_dw, w_out, mix_norm_post, mlp_norm_pre,
              w_mlp1, w_mlp2, mlp_norm_post):
    b, s, _ = x.shape
    o1 = N_ATT_COLS
    o2 = o1 + N_CONF_COLS
    o3 = o2 + N_POOL_COLS
    for l in range(DEPTH):
        h = rmsnorm(x, mix_norm_pre[l])
        proj = jnp.einsum("bsd,de->bse", h, w_in[l])
        att_in, conf_in, pool_in, sc_in = proj[..., :o1], proj[..., o1:o2], proj[..., o2:o3], proj[..., o3:]

        q = att_in[..., :GROUP_W].reshape(b, s, ATT_HEADS, HEAD_DIM)
        k = att_in[..., GROUP_W:2 * GROUP_W].reshape(b, s, ATT_HEADS, HEAD_DIM)
        v = att_in[..., 2 * GROUP_W:3 * GROUP_W].reshape(b, s, ATT_HEADS, HEAD_DIM)
        log_f = jax.nn.log_sigmoid(att_in[..., 3 * GROUP_W:].astype(jnp.float32)
                                   + b_forget[l].astype(jnp.float32))
        y_att = forgetting_attention(q, k, v, log_f).reshape(b, s, GROUP_W)

        a, g = jnp.split(conf_in, 2, axis=-1)
        c = causal_depthwise_conv(a * jax.nn.sigmoid(g), conf_dw[l])
        y_conf = jax.nn.silu(layernorm(c, conf_ln_g[l], conf_ln_b[l]))

        y_pool = multiscale_pool(pool_in, pool_w[l], pool_scale[l])

        bg, cg, hs = jnp.split(sc_in, 3, axis=-1)
        y_sc = bg * causal_depthwise_conv(cg * hs, sc_dw[l])

        y = jnp.concatenate([y_att, y_conf, y_pool, y_sc], axis=-1)
        y = jnp.einsum("bse,ed->bsd", y, w_out[l])
        x = x + rmsnorm(y, mix_norm_post[l])

        h = rmsnorm(x, mlp_norm_pre[l])
        u = jnp.square(jax.nn.relu(jnp.einsum("bsd,df->bsf", h, w_mlp1[l])))
        y = jnp.einsum("bsf,fd->bsd", u, w_mlp2[l])
        x = x + rmsnorm(y, mlp_norm_post[l])
    return x
```

```python
import functools

import jax
import jax.numpy as jnp
from jax import lax
from jax.experimental import pallas as pl
from jax.experimental.pallas import tpu as pltpu

F32 = jnp.float32
BF16 = jnp.bfloat16

EPS = 1e-6
N_HEADS = 4
HEAD_DIM = 128
GROUP_W = N_HEADS * HEAD_DIM
CONF_K = 31
SC_K = 3
POOL_WINDOWS = (2, 4, 8, 16)
POOL_GROUP = GROUP_W // len(POOL_WINDOWS)
LANES = 128
HALO = 32
N_LOCAL_COLS = 6 * GROUP_W
N_MAIN_COLS = 9 * GROUP_W
VMEM_LIMIT = 56 * 1024 * 1024

TM_IN, TN_IN = 512, 768
TM_OUT = 512
TM_MLP, TF_MLP = 512, 1024
TS_LOCAL = 512
ROWS = 32
TQ = 256


def _params(*sem):
    return pltpu.CompilerParams(dimension_semantics=sem, vmem_limit_bytes=VMEM_LIMIT)


def _rmsnorm_chunks(src_ref, gain_ref, rows, emit):
    def body(c, carry):
        r = pl.multiple_of(c * ROWS, ROWS)
        x = src_ref[pl.ds(r, ROWS), :]
        ms = jnp.mean(x * x, axis=-1, keepdims=True)
        emit(r, x * lax.rsqrt(ms + EPS) * gain_ref[...])
        return carry
    lax.fori_loop(0, rows // ROWS, body, 0)


def _in_proj_kernel(x_ref, g_ref, w_ref, wg_ref, proj_ref, gate_ref, h_ref):
    @pl.when(pl.program_id(1) == 0)
    def _():
        def emit(r, y):
            h_ref[pl.ds(r, ROWS), :] = y.astype(BF16)
        _rmsnorm_chunks(x_ref, g_ref, x_ref.shape[0], emit)
        gate_ref[...] = jnp.dot(h_ref[...], wg_ref[...], preferred_element_type=F32)

    proj_ref[...] = jnp.dot(h_ref[...], w_ref[...], preferred_element_type=F32).astype(BF16)


def _in_proj(x, gain, w_main, w_gate):
    t, d = x.shape
    tm = min(TM_IN, t)
    return pl.pallas_call(
        _in_proj_kernel,
        grid=(t // tm, N_MAIN_COLS // TN_IN),
        in_specs=[
            pl.BlockSpec((tm, d), lambda i, j: (i, 0)),
            pl.BlockSpec((1, d), lambda i, j: (0, 0)),
            pl.BlockSpec((d, TN_IN), lambda i, j: (0, j)),
            pl.BlockSpec((d, LANES), lambda i, j: (0, 0)),
        ],
        out_specs=[
            pl.BlockSpec((tm, TN_IN), lambda i, j: (i, j)),
            pl.BlockSpec((tm, LANES), lambda i, j: (i, 0)),
        ],
        out_shape=[
            jax.ShapeDtypeStruct((t, N_MAIN_COLS), BF16),
            jax.ShapeDtypeStruct((t, LANES), F32),
        ],
        scratch_shapes=[pltpu.VMEM((tm, d), BF16)],
        compiler_params=_params("parallel", "arbitrary"),
        name="in_proj",
    )(x, gain, w_main, w_gate)


def _split3(x):
    hi = x.astype(BF16)
    r1 = x - hi.astype(F32)
    mid = r1.astype(BF16)
    lo = (r1 - mid.astype(F32)).astype(BF16)
    return hi, mid, lo


def _gate_kernel(g_ref, b_ref, fcol_ref, frow_ref):
    s = g_ref.shape[0]
    row = lax.broadcasted_iota(jnp.int32, (LANES, LANES), 0)
    col = lax.broadcasted_iota(jnp.int32, (LANES, LANES), 1)
    tri = jnp.where(row >= col, 1.0, 0.0).astype(BF16)
    carry = jnp.zeros((1, LANES), F32)
    for j in range(s // LANES):
        z = g_ref[j * LANES:(j + 1) * LANES, :] + b_ref[...]
        logf = jnp.minimum(z, 0.0) - jnp.log1p(jnp.exp(-jnp.abs(z)))
        hi, mid, lo = _split3(logf)
        cum = (jnp.dot(tri, hi, preferred_element_type=F32)
               + jnp.dot(tri, mid, preferred_element_type=F32)
               + jnp.dot(tri, lo, preferred_element_type=F32)) + carry
        carry = cum[LANES - 1:LANES, :]
        fcol_ref[j * LANES:(j + 1) * LANES, :] = cum
        frow_ref[:, j * LANES:(j + 1) * LANES] = cum.T[0:8, :]


def _gate_prefix(gate, bias, b, s):
    return pl.pallas_call(
        _gate_kernel,
        grid=(b,),
        in_specs=[
            pl.BlockSpec((s, LANES), lambda i: (i, 0)),
            pl.BlockSpec((1, LANES), lambda i: (0, 0)),
        ],
        out_specs=[
            pl.BlockSpec((s, LANES), lambda i: (i, 0)),
            pl.BlockSpec((None, 8, s), lambda i: (i, 0, 0)),
        ],
        out_shape=[
            jax.ShapeDtypeStruct((b * s, LANES), F32),
            jax.ShapeDtypeStruct((b, 8, s), F32),
        ],
        compiler_params=_params("parallel"),
        name="gate_prefix",
    )(gate, bias)


def _attn_kernel(q_ref, k_ref, v_ref, fcol_ref, frow_ref, o_ref, s_ref, p_ref):
    s_len = q_ref.shape[0]
    h = pl.program_id(1)
    scale = 1.0 / (HEAD_DIM ** 0.5)
    lane = lax.broadcasted_iota(jnp.int32, (s_len, LANES), 1)
    fq_all = jnp.sum(jnp.where(lane == h, fcol_ref[...], 0.0), axis=1, keepdims=True)
    fk_all = frow_ref[pl.ds(h, 1), :]
    nt = (((1,), (1,)), ((), ()))
    for i in range(s_len // TQ):
        q0 = i * TQ
        q = q_ref[q0:q0 + TQ, :]
        fq = fq_all[q0:q0 + TQ, :]
        m_part = jnp.full((TQ, LANES), -jnp.inf, F32)
        for c in range(i + 1):
            c0 = c * TQ
            sc = lax.dot_general(q, k_ref[c0:c0 + TQ, :], nt, preferred_element_type=F32)
            sc = sc * scale + (fq - fk_all[:, c0:c0 + TQ])
            if c == i:
                r_id = lax.broadcasted_iota(jnp.int32, (TQ, TQ), 0)
                c_id = lax.broadcasted_iota(jnp.int32, (TQ, TQ), 1)
                sc = jnp.where(c_id <= r_id, sc, -jnp.inf)
            s_ref[:, c0:c0 + TQ] = sc
            for t in range(TQ // LANES):
                m_part = jnp.maximum(m_part, sc[:, t * LANES:(t + 1) * LANES])
        m = jnp.max(m_part, axis=1, keepdims=True)
        l_part = jnp.zeros((TQ, LANES), F32)
        for c in range(i + 1):
            c0 = c * TQ
            p = jnp.exp(s_ref[:, c0:c0 + TQ] - m)
            for t in range(TQ // LANES):
                l_part = l_part + p[:, t * LANES:(t + 1) * LANES]
            p_ref[:, c0:c0 + TQ] = p.astype(BF16)
        l = jnp.sum(l_part, axis=1, keepdims=True)
        kv = q0 + TQ
        o = jnp.dot(p_ref[:, 0:kv], v_ref[0:kv, :], preferred_element_type=F32)
        o_ref[q0:q0 + TQ, :] = (o / l).astype(BF16)


def _attention(proj, fcol, frow, b, s):
    t = b * s
    qb = N_LOCAL_COLS // HEAD_DIM
    return pl.pallas_call(
        _attn_kernel,
        grid=(b, N_HEADS),
        in_specs=[
            pl.BlockSpec((s, HEAD_DIM), lambda i, h: (i, qb + h)),
            pl.BlockSpec((s, HEAD_DIM), lambda i, h: (i, qb + N_HEADS + h)),
            pl.BlockSpec((s, HEAD_DIM), lambda i, h: (i, qb + 2 * N_HEADS + h)),
            pl.BlockSpec((s, LANES), lambda i, h: (i, 0)),
            pl.BlockSpec((None, 8, s), lambda i, h: (i, 0, 0)),
        ],
        out_specs=pl.BlockSpec((s, HEAD_DIM), lambda i, h: (i, h)),
        out_shape=jax.ShapeDtypeStruct((t, GROUP_W), BF16),
        scratch_shapes=[pltpu.VMEM((TQ, s), F32), pltpu.VMEM((TQ, s), BF16)],
        compiler_params=_params("parallel", "arbitrary"),
        name="fox_attention",
    )(proj, proj, proj, fcol, frow)


def _local_kernel(main_ref, halo_ref, cw_ref, lng_ref, lnb_ref, pw_ref, ps_ref, sw_ref,
                  o_ref, u_ref, p_ref, z_ref, pooled_ref):
    ts = main_ref.shape[0]
    i = pl.program_id(1)
    g = GROUP_W

    def fill(dst0, src):
        src = src.astype(F32)
        a, gt = src[:, 0:g], src[:, g:2 * g]
        u_ref[dst0:dst0 + src.shape[0], :] = a * jax.nn.sigmoid(gt)
        p_ref[dst0:dst0 + src.shape[0], :] = src[:, 2 * g:3 * g]
        z_ref[dst0:dst0 + src.shape[0], :] = src[:, 4 * g:5 * g] * src[:, 5 * g:6 * g]

    halo = jnp.where(i > 0, halo_ref[...], jnp.zeros_like(halo_ref))
    fill(0, halo)
    for r in range(0, ts, ROWS):
        fill(HALO + r, main_ref[r:r + ROWS, :])

    row_id = lax.broadcasted_iota(jnp.int32, (ROWS, POOL_GROUP), 0)
    for r in range(0, ts, ROWS):
        e0 = HALO + r
        acc = jnp.zeros((ROWS, g), F32)
        for k in range(CONF_K):
            off = e0 - (CONF_K - 1) + k
            acc = acc + cw_ref[k:k + 1, :] * u_ref[off:off + ROWS, :]
        mu = jnp.mean(acc, axis=-1, keepdims=True)
        cen = acc - mu
        var = jnp.mean(cen * cen, axis=-1, keepdims=True)
        y = cen * lax.rsqrt(var + EPS) * lng_ref[...] + lnb_ref[...]
        o_ref[r:r + ROWS, 0:g] = (y * jax.nn.sigmoid(y)).astype(BF16)

        pos = i * ts + r + row_id
        for gi, w in enumerate(POOL_WINDOWS):
            c0 = gi * POOL_GROUP
            xt = p_ref[e0:e0 + ROWS, c0:c0 + POOL_GROUP]
            win = xt
            for jj in range(1, w):
                win = win + p_ref[e0 - jj:e0 - jj + ROWS, c0:c0 + POOL_GROUP]
            cnt = jnp.minimum(pos + 1, w).astype(F32)
            pooled_ref[r:r + ROWS, c0:c0 + POOL_GROUP] = (win / cnt - xt).astype(BF16)

        conv = jnp.zeros((ROWS, g), F32)
        for k in range(SC_K):
            off = e0 - (SC_K - 1) + k
            conv = conv + sw_ref[k:k + 1, :] * z_ref[off:off + ROWS, :]
        bg = main_ref[r:r + ROWS, 3 * g:4 * g].astype(F32)
        o_ref[r:r + ROWS, 2 * g:3 * g] = (bg * conv).astype(BF16)

    for gi in range(len(POOL_WINDOWS)):
        c0 = gi * POOL_GROUP
        y = jnp.dot(pooled_ref[:, c0:c0 + POOL_GROUP], pw_ref[gi], preferred_element_type=F32)
        o_ref[:, g + c0:g + c0 + POOL_GROUP] = (y * ps_ref[:, c0:c0 + POOL_GROUP]).astype(BF16)


def _local_mixers(proj, conf_dw, ln_g, ln_b, pool_w, pool_scale, sc_dw, b, s):
    t = b * s
    ts = min(TS_LOCAL, s)
    nst = s // ts
    hb = ts // HALO
    g = GROUP_W

    def halo_idx(bi, si):
        return (jnp.maximum(bi * nst * hb + si * hb - 1, 0), 0)

    full = lambda shape: pl.BlockSpec(shape, lambda bi, si: (0,) * len(shape))
    return pl.pallas_call(
        _local_kernel,
        grid=(b, nst),
        in_specs=[
            pl.BlockSpec((ts, N_LOCAL_COLS), lambda bi, si: (bi * nst + si, 0)),
            pl.BlockSpec((HALO, N_LOCAL_COLS), halo_idx),
            full((CONF_K, g)), full((1, g)), full((1, g)),
            full((len(POOL_WINDOWS), POOL_GROUP, POOL_GROUP)), full((1, g)), full((SC_K, g)),
        ],
        out_specs=pl.BlockSpec((ts, 3 * g), lambda bi, si: (bi * nst + si, 0)),
        out_shape=jax.ShapeDtypeStruct((t, 3 * g), BF16),
        scratch_shapes=[pltpu.VMEM((HALO + ts, g), F32)] * 3 + [pltpu.VMEM((ts, g), BF16)],
        compiler_params=_params("parallel", "parallel"),
        name="local_mixers",
    )(proj, proj, conf_dw, ln_g, ln_b, pool_w, pool_scale, sc_dw)


def _out_proj_kernel(ya_ref, yl_ref, w_ref, x_ref, g_ref, o_ref):
    g = GROUP_W
    o_ref[...] = (jnp.dot(ya_ref[...], w_ref[0:g, :], preferred_element_type=F32)
                  + jnp.dot(yl_ref[...], w_ref[g:, :], preferred_element_type=F32))

    def emit(r, y):
        o_ref[pl.ds(r, ROWS), :] = x_ref[pl.ds(r, ROWS), :] + y
    _rmsnorm_chunks(o_ref, g_ref, o_ref.shape[0], emit)


def _out_proj(y_att, y_loc, w_out, x, gain):
    t, d = x.shape
    tm = min(TM_OUT, t)
    return pl.pallas_call(
        _out_proj_kernel,
        grid=(t // tm,),
        in_specs=[
            pl.BlockSpec((tm, GROUP_W), lambda i: (i, 0)),
            pl.BlockSpec((tm, 3 * GROUP_W), lambda i: (i, 0)),
            pl.BlockSpec((d, d), lambda i: (0, 0)),
            pl.BlockSpec((tm, d), lambda i: (i, 0)),
            pl.BlockSpec((1, d), lambda i: (0, 0)),
        ],
        out_specs=pl.BlockSpec((tm, d), lambda i: (i, 0)),
        out_shape=jax.ShapeDtypeStruct((t, d), F32),
        compiler_params=_params("parallel"),
        name="out_proj",
    )(y_att, y_loc, w_out, x, gain)


def _mlp_kernel(x_ref, gpre_ref, w1_ref, w2_ref, gpost_ref, o_ref, h_ref):
    f = pl.program_id(1)

    @pl.when(f == 0)
    def _():
        def emit(r, y):
            h_ref[pl.ds(r, ROWS), :] = y.astype(BF16)
        _rmsnorm_chunks(x_ref, gpre_ref, x_ref.shape[0], emit)

    u = jnp.dot(h_ref[...], w1_ref[...], preferred_element_type=F32)
    u = jnp.square(jnp.maximum(u, 0.0)).astype(BF16)
    y = jnp.dot(u, w2_ref[...], preferred_element_type=F32)

    @pl.when(f == 0)
    def _():
        o_ref[...] = y

    @pl.when(f > 0)
    def _():
        o_ref[...] += y

    @pl.when(f == pl.num_programs(1) - 1)
    def _():
        def emit(r, yn):
            o_ref[pl.ds(r, ROWS), :] = x_ref[pl.ds(r, ROWS), :] + yn
        _rmsnorm_chunks(o_ref, gpost_ref, o_ref.shape[0], emit)


def _mlp(x, gpre, w1, w2, gpost):
    t, d = x.shape
    dff = w1.shape[1]
    tm = min(TM_MLP, t)
    return pl.pallas_call(
        _mlp_kernel,
        grid=(t // tm, dff // TF_MLP),
        in_specs=[
            pl.BlockSpec((tm, d), lambda i, f: (i, 0)),
            pl.BlockSpec((1, d), lambda i, f: (0, 0)),
            pl.BlockSpec((d, TF_MLP), lambda i, f: (0, f)),
            pl.BlockSpec((TF_MLP, d), lambda i, f: (f, 0)),
            pl.BlockSpec((1, d), lambda i, f: (0, 0)),
        ],
        out_specs=pl.BlockSpec((tm, d), lambda i, f: (i, 0)),
        out_shape=jax.ShapeDtypeStruct((t, d), F32),
        scratch_shapes=[pltpu.VMEM((tm, d), BF16)],
        compiler_params=_params("parallel", "arbitrary"),
        name="mlp",
    )(x, gpre, w1, w2, gpost)


def kernel(x, mix_norm_pre, w_in, b_forget, conf_dw, conf_ln_g, conf_ln_b, pool_w, pool_scale,
           sc_dw, w_out, mix_norm_post, mlp_norm_pre, w_mlp1, w_mlp2, mlp_norm_post):
    b, s, d = x.shape
    depth = w_in.shape[0]
    g = GROUP_W
    o_gate = 3 * g
    o_conf = o_gate + N_HEADS
    w_main = jnp.concatenate([w_in[:, :, o_conf:], w_in[:, :, :o_gate]], axis=-1).astype(BF16)
    w_gate = jnp.pad(w_in[:, :, o_gate:o_conf], ((0, 0), (0, 0), (0, LANES - N_HEADS))).astype(BF16)
    b_gate = jnp.pad(b_forget.astype(F32), ((0, 0), (0, LANES - N_HEADS)))[:, None, :]
    w_out_b = w_out.astype(BF16)
    w1_b = w_mlp1.astype(BF16)
    w2_b = w_mlp2.astype(BF16)
    pool_w_b = pool_w.astype(BF16)
    row = lambda a: a.astype(F32)[:, None, :]
    n_pre, n_post, m_pre, m_post = row(mix_norm_pre), row(mix_norm_post), row(mlp_norm_pre), row(mlp_norm_post)
    ln_g, ln_b, p_scale = row(conf_ln_g), row(conf_ln_b), row(pool_scale)

    xt = x.reshape(b * s, d)
    for l in range(depth):
        proj, gate = _in_proj(xt, n_pre[l], w_main[l], w_gate[l])
        fcol, frow = _gate_prefix(gate, b_gate[l], b, s)
        y_att = _attention(proj, fcol, frow, b, s)
        y_loc = _local_mixers(proj, conf_dw[l], ln_g[l], ln_b[l], pool_w_b[l], p_scale[l], sc_dw[l], b, s)
        xt = _out_proj(y_att, y_loc, w_out_b[l], xt, n_post[l])
        xt = _mlp(xt, m_pre[l], w1_b[l], w2_b[l], m_post[l])
    return xt.reshape(b, s, d)
```

```python
import jax
import jax.numpy as jnp
from jax import lax
from jax.experimental import pallas as pl
from jax.experimental.pallas import tpu as pltpu

F32 = jnp.float32
BF16 = jnp.bfloat16

EPS = 1e-6
LOG2E = 1.4426950408889634
N_HEADS = 4
HEAD_DIM = 128
GROUP_W = N_HEADS * HEAD_DIM
CONF_K = 31
SC_K = 3
POOL_WINDOWS = (2, 4, 8, 16)
POOL_GROUP = GROUP_W // len(POOL_WINDOWS)
LANES = 128
SUBLANES = 8
HALO = 32
N_LOCAL_COLS = 6 * GROUP_W
N_MAIN_COLS = 9 * GROUP_W
VMEM_LIMIT = 56 * 1024 * 1024

TM_PREP = 512
TM_IN, TN_IN = 1024, 768
TM_OUT = 512
TM_MLP, TF_MLP = 512, 1024
SUB = 128
TS_LOCAL = 512
ROWS = 32
TQ = 256


def _params(*sem):
    return pltpu.CompilerParams(dimension_semantics=sem, vmem_limit_bytes=VMEM_LIMIT)


def _lanes(v, width):
    return jnp.concatenate([v] * (width // LANES), axis=1)


def _inv_rms(v):
    ms = jnp.mean(v * v, axis=-1, keepdims=True)
    return jnp.broadcast_to(lax.rsqrt(ms + EPS), (v.shape[0], LANES))


def _emit_stream(x_new, r0, x_ref, h_ref, rs_ref):
    rows = x_new.shape[0]
    x_ref[r0:r0 + rows, :] = x_new
    h_ref[r0:r0 + rows, :] = x_new.astype(BF16)
    rs_ref[r0:r0 + rows, :] = _inv_rms(x_new)


def _prep_kernel(x_ref, h_ref, rs_ref):
    for r0 in range(0, x_ref.shape[0], SUB):
        x = x_ref[r0:r0 + SUB, :]
        h_ref[r0:r0 + SUB, :] = x.astype(BF16)
        rs_ref[r0:r0 + SUB, :] = _inv_rms(x)


def _prep(x):
    t, d = x.shape
    tm = min(TM_PREP, t)
    return pl.pallas_call(
        _prep_kernel,
        grid=(t // tm,),
        in_specs=[pl.BlockSpec((tm, d), lambda i: (i, 0))],
        out_specs=[pl.BlockSpec((tm, d), lambda i: (i, 0)), pl.BlockSpec((tm, LANES), lambda i: (i, 0))],
        out_shape=[jax.ShapeDtypeStruct((t, d), BF16), jax.ShapeDtypeStruct((t, LANES), F32)],
        compiler_params=_params("parallel"),
        name="stream_prep",
    )(x)


def _in_proj_kernel(h_ref, rs_ref, w_ref, wg_ref, proj_ref, gate_ref):
    rs = rs_ref[...]

    @pl.when(pl.program_id(1) == 0)
    def _():
        gate_ref[...] = jnp.dot(h_ref[...], wg_ref[...], preferred_element_type=F32) * rs

    acc = jnp.dot(h_ref[...], w_ref[...], preferred_element_type=F32)
    proj_ref[...] = (acc * _lanes(rs, acc.shape[1])).astype(BF16)


def _in_proj(h, rs, w_main, w_gate):
    t, d = h.shape
    tm = min(TM_IN, t)
    nt, _, tn = w_main.shape
    return pl.pallas_call(
        _in_proj_kernel,
        grid=(t // tm, nt),
        in_specs=[
            pl.BlockSpec((tm, d), lambda i, j: (i, 0)),
            pl.BlockSpec((tm, LANES), lambda i, j: (i, 0)),
            pl.BlockSpec((None, d, tn), lambda i, j: (j, 0, 0)),
            pl.BlockSpec((d, LANES), lambda i, j: (0, 0)),
        ],
        out_specs=[
            pl.BlockSpec((tm, tn), lambda i, j: (i, j)),
            pl.BlockSpec((tm, LANES), lambda i, j: (i, 0)),
        ],
        out_shape=[
            jax.ShapeDtypeStruct((t, nt * tn), BF16),
            jax.ShapeDtypeStruct((t, LANES), F32),
        ],
        compiler_params=_params("parallel", "arbitrary"),
        name="in_proj",
    )(h, rs, w_main, w_gate)


def _split3(x):
    hi = x.astype(BF16)
    r1 = x - hi.astype(F32)
    mid = r1.astype(BF16)
    lo = (r1 - mid.astype(F32)).astype(BF16)
    return hi, mid, lo


def _gate_kernel(g_ref, b_ref, fcol_ref, frow_ref):
    s = g_ref.shape[0]
    row = lax.broadcasted_iota(jnp.int32, (LANES, LANES), 0)
    col = lax.broadcasted_iota(jnp.int32, (LANES, LANES), 1)
    tri = jnp.where(row >= col, 1.0, 0.0).astype(BF16)
    carry = jnp.zeros((1, LANES), F32)
    for j in range(s // LANES):
        z = g_ref[j * LANES:(j + 1) * LANES, :] + b_ref[...]
        logf = (jnp.minimum(z, 0.0) - jnp.log1p(jnp.exp(-jnp.abs(z)))) * LOG2E
        hi, mid, lo = _split3(logf)
        cum = (jnp.dot(tri, hi, preferred_element_type=F32)
               + jnp.dot(tri, mid, preferred_element_type=F32)
               + jnp.dot(tri, lo, preferred_element_type=F32)) + carry
        carry = cum[LANES - 1:LANES, :]
        fcol_ref[j * LANES:(j + 1) * LANES, :] = cum
        frow_ref[:, j * LANES:(j + 1) * LANES] = cum.T[0:8, :]


def _gate_prefix(gate, bias, b, s):
    return pl.pallas_call(
        _gate_kernel,
        grid=(b,),
        in_specs=[
            pl.BlockSpec((s, LANES), lambda i: (i, 0)),
            pl.BlockSpec((1, LANES), lambda i: (0, 0)),
        ],
        out_specs=[
            pl.BlockSpec((s, LANES), lambda i: (i, 0)),
            pl.BlockSpec((None, 8, s), lambda i: (i, 0, 0)),
        ],
        out_shape=[
            jax.ShapeDtypeStruct((b * s, LANES), F32),
            jax.ShapeDtypeStruct((b, 8, s), F32),
        ],
        compiler_params=_params("parallel"),
        name="gate_prefix",
    )(gate, bias)


def _attn_kernel(q_ref, k_ref, v_ref, fcol_ref, frow_ref, o_ref, s_ref, p_ref):
    s_len = q_ref.shape[0]
    h = pl.program_id(1)
    lane = lax.broadcasted_iota(jnp.int32, (s_len, LANES), 1)
    fq_all = jnp.sum(jnp.where(lane == h, fcol_ref[...], 0.0), axis=1, keepdims=True)
    fk_all = frow_ref[pl.ds(h, 1), :]
    nt = (((1,), (1,)), ((), ()))
    for i in range(s_len // TQ):
        q0 = i * TQ
        q = q_ref[q0:q0 + TQ, :]
        fq = fq_all[q0:q0 + TQ, :]
        m_part = jnp.full((TQ, LANES), -jnp.inf, F32)
        for c in range(i + 1):
            c0 = c * TQ
            sc = lax.dot_general(q, k_ref[c0:c0 + TQ, :], nt, preferred_element_type=F32)
            sc = sc + (fq - fk_all[:, c0:c0 + TQ])
            if c == i:
                r_id = lax.broadcasted_iota(jnp.int32, (TQ, TQ), 0)
                c_id = lax.broadcasted_iota(jnp.int32, (TQ, TQ), 1)
                sc = jnp.where(c_id <= r_id, sc, -jnp.inf)
            s_ref[:,c0:c0 + TQ] = sc
            for t in range(TQ // LANES):
                m_part = jnp.maximum(m_part, sc[:, t * LANES:(t + 1) * LANES])
        m = jnp.max(m_part, axis=1, keepdims=True)
        l_part = jnp.zeros((TQ, LANES), F32)
        for c in range(i + 1):
            c0 = c * TQ
            p = jnp.exp2(s_ref[:,c0:c0 + TQ] - m)
            for t in range(TQ // LANES):
                l_part = l_part + p[:, t * LANES:(t + 1) * LANES]
            p_ref[:,c0:c0 + TQ] = p.astype(BF16)
        l = jnp.sum(l_part, axis=1, keepdims=True)
        kv = q0 + TQ
        o = jnp.dot(p_ref[:,0:kv], v_ref[0:kv, :], preferred_element_type=F32)
        o_ref[q0:q0 + TQ, :] = (o / l).astype(BF16)


def _attention(proj, fcol, frow, b, s):
    t = b * s
    qb = N_LOCAL_COLS // HEAD_DIM
    return pl.pallas_call(
        _attn_kernel,
        grid=(b, N_HEADS),
        in_specs=[
            pl.BlockSpec((s, HEAD_DIM), lambda i, h: (i, qb + h)),
            pl.BlockSpec((s, HEAD_DIM), lambda i, h: (i, qb + N_HEADS + h)),
            pl.BlockSpec((s, HEAD_DIM), lambda i, h: (i, qb + 2 * N_HEADS + h)),
            pl.BlockSpec((s, LANES), lambda i, h: (i, 0)),
            pl.BlockSpec((None, 8, s), lambda i, h: (i, 0, 0)),
        ],
        out_specs=pl.BlockSpec((s, HEAD_DIM), lambda i, h: (i, h)),
        out_shape=jax.ShapeDtypeStruct((t, GROUP_W), BF16),
        scratch_shapes=[pltpu.VMEM((TQ, s), F32), pltpu.VMEM((TQ, s), BF16)],
        compiler_params=_params("parallel", "arbitrary"),
        name="fox_attention",
    )(proj, proj, proj, fcol, frow)


def _local_kernel(main_ref, halo_ref, cw_ref, lng_ref, lnb_ref, pw_ref, ps_ref, sw_ref,
                  o_ref, u_ref, p_ref, z_ref, pooled_ref):
    ts = main_ref.shape[0]
    i = pl.program_id(1)
    g = GROUP_W

    def rows(v):
        return jnp.concatenate([v] * (ROWS // SUBLANES), axis=0)

    def fill(dst0, src):
        src = src.astype(F32)
        a, gt = src[:, 0:g], src[:, g:2 * g]
        u_ref[0, dst0:dst0 + src.shape[0], :] = a * jax.nn.sigmoid(gt)
        p_ref[dst0:dst0 + src.shape[0], :] = src[:, 2 * g:3 * g]
        z_ref[dst0:dst0 + src.shape[0], :] = src[:, 4 * g:5 * g] * src[:, 5 * g:6 * g]

    halo = jnp.where(i > 0, halo_ref[...], jnp.zeros_like(halo_ref))
    fill(0, halo)
    for r in range(0, ts, ROWS):
        fill(HALO + r, main_ref[r:r + ROWS, :])
    n_ext = HALO + ts
    for sft in range(1, SUBLANES):
        for r in range(SUBLANES, n_ext, 4 * ROWS):
            r1 = min(r + 4 * ROWS, n_ext)
            u_ref[sft, r:r1, :] = u_ref[0, r - sft:r1 - sft, :]

    row_id = lax.broadcasted_iota(jnp.int32, (ROWS, POOL_GROUP), 0)
    for r in range(0, ts, ROWS):
        e0 = HALO + r
        acc = jnp.zeros((ROWS, g), F32)
        for k in range(CONF_K):
            back8, sft = divmod(CONF_K - 1 - k, SUBLANES)
            off = e0 - SUBLANES * back8
            acc = acc + rows(cw_ref[k]) * u_ref[sft, off:off + ROWS, :]
        mu = jnp.mean(acc, axis=-1, keepdims=True)
        cen = acc - mu
        var = jnp.mean(cen * cen, axis=-1, keepdims=True)
        y = cen * lax.rsqrt(var + EPS) * rows(lng_ref[...]) + rows(lnb_ref[...])
        o_ref[r:r + ROWS, 0:g] = (y * jax.nn.sigmoid(y)).astype(BF16)

        pos = i * ts + r + row_id
        for gi, w in enumerate(POOL_WINDOWS):
            c0 = gi * POOL_GROUP
            xt = p_ref[e0:e0 + ROWS, c0:c0 + POOL_GROUP]
            win = xt
            for jj in range(1, w):
                win = win + p_ref[e0 - jj:e0 - jj + ROWS, c0:c0 + POOL_GROUP]
            cnt = jnp.minimum(pos + 1, w).astype(F32)
            pooled_ref[r:r + ROWS, c0:c0 + POOL_GROUP] = (win / cnt - xt).astype(BF16)

        conv = jnp.zeros((ROWS, g), F32)
        for k in range(SC_K):
            off = e0 - (SC_K - 1) + k
            conv = conv + rows(sw_ref[k]) * z_ref[off:off + ROWS, :]
        bg = main_ref[r:r + ROWS, 3 * g:4 * g].astype(F32)
        o_ref[r:r + ROWS, 2 * g:3 * g] = (bg * conv).astype(BF16)

    for gi in range(len(POOL_WINDOWS)):
        c0 = gi * POOL_GROUP
        y = jnp.dot(pooled_ref[:, c0:c0 + POOL_GROUP], pw_ref[gi], preferred_element_type=F32)
        o_ref[:, g + c0:g + c0 + POOL_GROUP] = (y * ps_ref[:, c0:c0 + POOL_GROUP]).astype(BF16)


def _local_mixers(proj, conf_dw, ln_g, ln_b, pool_w, pool_scale, sc_dw, b, s):
    t = b * s
    ts = min(TS_LOCAL, s)
    nst = s // ts
    hb = ts // HALO
    g = GROUP_W

    def halo_idx(bi, si):
        return (jnp.maximum(bi * nst * hb + si * hb - 1, 0), 0)

    full = lambda shape: pl.BlockSpec(shape, lambda bi, si: (0,) * len(shape))
    return pl.pallas_call(
        _local_kernel,
        grid=(b, nst),
        in_specs=[
            pl.BlockSpec((ts, N_LOCAL_COLS), lambda bi, si: (bi * nst + si, 0)),
            pl.BlockSpec((HALO, N_LOCAL_COLS), halo_idx),
            full((CONF_K, SUBLANES, g)), full((SUBLANES, g)), full((SUBLANES, g)),
            full((len(POOL_WINDOWS), POOL_GROUP, POOL_GROUP)), full((1, g)), full((SC_K, SUBLANES, g)),
        ],
        out_specs=pl.BlockSpec((ts, 3 * g), lambda bi, si: (bi * nst + si, 0)),
        out_shape=jax.ShapeDtypeStruct((t, 3 * g), BF16),
        scratch_shapes=[pltpu.VMEM((SUBLANES, HALO + ts, g), F32)] + [pltpu.VMEM((HALO + ts, g), F32)] * 2
        + [pltpu.VMEM((ts, g), BF16)],
        compiler_params=_params("parallel", "parallel"),
        name="local_mixers",
    )(proj, proj, conf_dw, ln_g, ln_b, pool_w, pool_scale, sc_dw)


def _post_norm_residual(y, x, gain):
    d = y.shape[1]
    return x + y * _lanes(_inv_rms(y), d) * gain


def _out_proj_kernel(ya_ref, yl_ref, w_ref, x_ref, g_ref, xo_ref, ho_ref, rso_ref):
    g = GROUP_W
    for r0 in range(0, x_ref.shape[0], SUB):
        y = (jnp.dot(ya_ref[r0:r0 + SUB, :], w_ref[0:g, :], preferred_element_type=F32)
             + jnp.dot(yl_ref[r0:r0 + SUB, :], w_ref[g:, :], preferred_element_type=F32))
        x_new = _post_norm_residual(y, x_ref[r0:r0 + SUB, :], g_ref[...])
        _emit_stream(x_new, r0, xo_ref, ho_ref, rso_ref)


def _stream_out(t, d, tm, idx):
    specs = [pl.BlockSpec((tm, d), idx), pl.BlockSpec((tm, d), idx), pl.BlockSpec((tm, LANES), idx)]
    shapes = [jax.ShapeDtypeStruct((t, d), F32), jax.ShapeDtypeStruct((t, d), BF16),
              jax.ShapeDtypeStruct((t, LANES), F32)]
    return specs, shapes


def _out_proj(y_att, y_loc, w_out, x, gain):
    t, d = x.shape
    tm = min(TM_OUT, t)
    out_specs, out_shape = _stream_out(t, d, tm, lambda i: (i, 0))
    return pl.pallas_call(
        _out_proj_kernel,
        grid=(t // tm,),
        in_specs=[
            pl.BlockSpec((tm, GROUP_W), lambda i: (i, 0)),
            pl.BlockSpec((tm, 3 * GROUP_W), lambda i: (i, 0)),
            pl.BlockSpec((d, d), lambda i: (0, 0)),
            pl.BlockSpec((tm, d), lambda i: (i, 0)),
            pl.BlockSpec((1, d), lambda i: (0, 0)),
        ],
        out_specs=out_specs,
        out_shape=out_shape,
        compiler_params=_params("parallel"),
        name="out_proj",
    )(y_att, y_loc, w_out, x, gain)


def _mlp_kernel(x_ref, h_ref, rs_ref, w1_ref, w2_ref, g_ref, xo_ref, ho_ref, rso_ref):
    f = pl.program_id(1)

    @pl.when(f == 0)
    def _():
        xo_ref[...] = jnp.zeros_like(xo_ref)

    u = jnp.dot(h_ref[...], w1_ref[...], preferred_element_type=F32)
    u = jnp.square(jnp.maximum(u, 0.0)).astype(BF16)
    xo_ref[...] += jnp.dot(u, w2_ref[...], preferred_element_type=F32)

    @pl.when(f == pl.num_programs(1) - 1)
    def _():
        d = xo_ref.shape[1]
        for r0 in range(0, x_ref.shape[0], SUB):
            rs = rs_ref[r0:r0 + SUB, :]
            y = xo_ref[r0:r0 + SUB, :] * _lanes(rs * rs, d)
            x_new = _post_norm_residual(y, x_ref[r0:r0 + SUB, :], g_ref[...])
            _emit_stream(x_new, r0, xo_ref, ho_ref, rso_ref)


def _mlp(x, h, rs, w1, w2, gain):
    t, d = x.shape
    nf, _, tf = w1.shape
    tm = min(TM_MLP, t)
    out_specs, out_shape = _stream_out(t, d, tm, lambda i, f: (i, 0))
    return pl.pallas_call(
        _mlp_kernel,
        grid=(t // tm, nf),
        in_specs=[
            pl.BlockSpec((tm, d), lambda i, f: (i, 0)),
            pl.BlockSpec((tm, d), lambda i, f: (i, 0)),
            pl.BlockSpec((tm, LANES), lambda i, f: (i, 0)),
            pl.BlockSpec((None, d, tf), lambda i, f: (f, 0, 0)),
            pl.BlockSpec((tf, d), lambda i, f: (f, 0)),
            pl.BlockSpec((1, d), lambda i, f: (0, 0)),
        ],
        out_specs=out_specs,
        out_shape=out_shape,
        compiler_params=_params("parallel", "arbitrary"),
        name="mlp",
    )(x, h, rs, w1, w2, gain)


def _col_tiles(w, tn):
    dep, d, n = w.shape
    return w.reshape(dep, d, n // tn, tn).transpose(0, 2, 1, 3)


def kernel(x, mix_norm_pre, w_in, b_forget, conf_dw, conf_ln_g, conf_ln_b, pool_w, pool_scale,
           sc_dw, w_out, mix_norm_post, mlp_norm_pre, w_mlp1, w_mlp2, mlp_norm_post):
    b, s, d = x.shape
    depth = w_in.shape[0]
    g = GROUP_W
    o_gate = 3 * g
    o_conf = o_gate + N_HEADS
    w_in_g = w_in * mix_norm_pre.astype(F32)[:, :, None]
    w_in_g = w_in_g.at[:, :, :g].multiply(LOG2E / (HEAD_DIM ** 0.5))
    w_main = _col_tiles(jnp.concatenate([w_in_g[:, :, o_conf:], w_in_g[:, :, :o_gate]], axis=-1).astype(BF16), TN_IN)
    w_gate = jnp.pad(w_in_g[:, :, o_gate:o_conf], ((0, 0), (0, 0), (0, LANES - N_HEADS))).astype(BF16)
    b_gate = jnp.pad(b_forget.astype(F32), ((0, 0), (0, LANES - N_HEADS)))[:, None, :]
    w_out_b = w_out.astype(BF16)
    w1_b = _col_tiles((w_mlp1 * mlp_norm_pre.astype(F32)[:, :, None]).astype(BF16), TF_MLP)
    w2_b = w_mlp2.astype(BF16)
    pool_w_b = pool_w.astype(BF16)
    row = lambda a: a.astype(F32)[:, None, :]
    rep = lambda a: jnp.broadcast_to(a.astype(F32)[..., None, :], a.shape[:-1] + (SUBLANES, a.shape[-1]))
    n_post, m_post = row(mix_norm_post), row(mlp_norm_post)
    ln_g, ln_b, p_scale = rep(conf_ln_g), rep(conf_ln_b), row(pool_scale)
    conf_dw, sc_dw = rep(conf_dw), rep(sc_dw)

    xt = x.reshape(b * s, d)
    h, rs = _prep(xt)
    for l in range(depth):
        proj, gate = _in_proj(h, rs, w_main[l], w_gate[l])
        fcol, frow = _gate_prefix(gate, b_gate[l], b, s)
        y_att = _attention(proj, fcol, frow, b, s)
        y_loc = _local_mixers(proj, conf_dw[l], ln_g[l], ln_b[l], pool_w_b[l], p_scale[l], sc_dw[l], b, s)
        xt, h, rs = _out_proj(y_att, y_loc, w_out_b[l], xt, n_post[l])
        xt, h, rs = _mlp(xt, h, rs, w1_b[l], w2_b[l], m_post[l])
    return xt.reshape(b, s, d)
```

```python
import jax
import jax.numpy as jnp
from jax import lax
from jax.experimental import pallas as pl
from jax.experimental.pallas import tpu as pltpu

F32 = jnp.float32
BF16 = jnp.bfloat16

EPS = 1e-6
LOG2E = 1.4426950408889634
N_HEADS = 4
HEAD_DIM = 128
GROUP_W = N_HEADS * HEAD_DIM
CONF_K = 31
SC_K = 3
POOL_WINDOWS = (2, 4, 8, 16)
POOL_GROUP = GROUP_W // len(POOL_WINDOWS)
LANES = 128
SUBLANES = 8
HALO = 32
N_LOCAL_COLS = 6 * GROUP_W
N_MAIN_COLS = 9 * GROUP_W
VMEM_LIMIT = 56 * 1024 * 1024

TM_IN, TN_IN = 1024, 768
TM_OUT = 512
TM_MLP, TF_MLP = 512, 1024
SUB = 128
TS_LOCAL = 512
ROWS = 32
TQ = 256


def _params(*sem):
    return pltpu.CompilerParams(dimension_semantics=sem, vmem_limit_bytes=VMEM_LIMIT)


def _lanes(v, width):
    return jnp.concatenate([v] * (width // LANES), axis=1)


def _inv_rms(v):
    ms = jnp.mean(v * v, axis=-1, keepdims=True)
    return jnp.broadcast_to(lax.rsqrt(ms + EPS), (v.shape[0], LANES))


def _emit_stream(x_new, r0, x_ref, h_ref, rs_ref):
    rows = x_new.shape[0]
    x_ref[r0:r0 + rows, :] = x_new
    h_ref[r0:r0 + rows, :] = x_new.astype(BF16)
    rs_ref[r0:r0 + rows, :] = _inv_rms(x_new)


def _in_proj_core(h_ref, rs_ref, w_ref, wg_ref, proj_ref, gate_ref):
    rs = rs_ref[...]

    @pl.when(pl.program_id(1) == 0)
    def _():
        gate_ref[...] = jnp.dot(h_ref[...], wg_ref[...], preferred_element_type=F32) * rs

    acc = jnp.dot(h_ref[...], w_ref[...], preferred_element_type=F32)
    proj_ref[...] = (acc * _lanes(rs, acc.shape[1])).astype(BF16)


def _in_proj_kernel(h_ref, rs_ref, w_ref, wg_ref, proj_ref, gate_ref):
    _in_proj_core(h_ref, rs_ref, w_ref, wg_ref, proj_ref, gate_ref)


def _in_proj_f32_kernel(x_ref, w_ref, wg_ref, proj_ref, gate_ref, h_ref, rs_ref):
    @pl.when(pl.program_id(1) == 0)
    def _():
        for r0 in range(0, x_ref.shape[0], SUB):
            x = x_ref[r0:r0 + SUB, :]
            h_ref[r0:r0 + SUB, :] = x.astype(BF16)
            rs_ref[r0:r0 + SUB, :] = _inv_rms(x)

    _in_proj_core(h_ref, rs_ref, w_ref, wg_ref, proj_ref, gate_ref)


def _in_proj(stream, w_main, w_gate):
    t, d = stream[0].shape
    tm = min(TM_IN, t)
    n = w_main.shape[1]
    first = len(stream) == 1
    stream_specs = [pl.BlockSpec((tm, d), lambda i, j: (i, 0))]
    if not first:
        stream_specs.append(pl.BlockSpec((tm, LANES), lambda i, j: (i, 0)))
    return pl.pallas_call(
        _in_proj_f32_kernel if first else _in_proj_kernel,
        grid=(t // tm, n // TN_IN),
        in_specs=stream_specs + [
            pl.BlockSpec((d, TN_IN), lambda i, j: (0, j)),
            pl.BlockSpec((d, LANES), lambda i, j: (0, 0)),
        ],
        out_specs=[
            pl.BlockSpec((tm, TN_IN), lambda i, j: (i, j)),
            pl.BlockSpec((tm, LANES), lambda i, j: (i, 0)),
        ],
        out_shape=[
            jax.ShapeDtypeStruct((t, n), BF16),
            jax.ShapeDtypeStruct((t, LANES), F32),
        ],
        scratch_shapes=[pltpu.VMEM((tm, d), BF16), pltpu.VMEM((tm, LANES), F32)] if first else [],
        compiler_params=_params("parallel", "arbitrary"),
        name="in_proj",
    )(*stream, w_main, w_gate)


def _split3(x):
    hi = x.astype(BF16)
    r1 = x - hi.astype(F32)
    mid = r1.astype(BF16)
    lo = (r1 - mid.astype(F32)).astype(BF16)
    return hi, mid, lo


def _gate_kernel(g_ref, b_ref, fcol_ref, frow_ref):
    s = g_ref.shape[0]
    row = lax.broadcasted_iota(jnp.int32, (LANES, LANES), 0)
    col = lax.broadcasted_iota(jnp.int32, (LANES, LANES), 1)
    tri = jnp.where(row >= col, 1.0, 0.0).astype(BF16)
    carry = jnp.zeros((1, LANES), F32)
    for j in range(s // LANES):
        z = g_ref[j * LANES:(j + 1) * LANES, :] + b_ref[...]
        logf = (jnp.minimum(z, 0.0) - jnp.log1p(jnp.exp(-jnp.abs(z)))) * LOG2E
        hi, mid, lo = _split3(logf)
        cum = (jnp.dot(tri, hi, preferred_element_type=F32)
               + jnp.dot(tri, mid, preferred_element_type=F32)
               + jnp.dot(tri, lo, preferred_element_type=F32)) + carry
        carry = cum[LANES - 1:LANES, :]
        fcol_ref[j * LANES:(j + 1) * LANES, :] = cum
        frow_ref[:, j * LANES:(j + 1) * LANES] = cum.T[0:8, :]


def _gate_prefix(gate, bias, b, s):
    return pl.pallas_call(
        _gate_kernel,
        grid=(b,),
        in_specs=[
            pl.BlockSpec((s, LANES), lambda i: (i, 0)),
            pl.BlockSpec((1, LANES), lambda i: (0, 0)),
        ],
        out_specs=[
            pl.BlockSpec((s, LANES), lambda i: (i, 0)),
            pl.BlockSpec((None, 8, s), lambda i: (i, 0, 0)),
        ],
        out_shape=[
            jax.ShapeDtypeStruct((b * s, LANES), F32),
            jax.ShapeDtypeStruct((b, 8, s), F32),
        ],
        compiler_params=_params("parallel"),
        name="gate_prefix",
    )(gate, bias)


def _attn_kernel(q_ref, k_ref, v_ref, fcol_ref, frow_ref, o_ref, s_ref, p_ref):
    s_len = q_ref.shape[0]
    h = pl.program_id(1)
    lane = lax.broadcasted_iota(jnp.int32, (s_len, LANES), 1)
    fq_all = jnp.sum(jnp.where(lane == h, fcol_ref[...], 0.0), axis=1, keepdims=True)
    fk_all = frow_ref[pl.ds(h, 1), :]
    nt = (((1,), (1,)), ((), ()))
    for i in range(s_len // TQ):
        q0 = i * TQ
        q = q_ref[q0:q0 + TQ, :]
        fq = fq_all[q0:q0 + TQ, :]
        m_part = jnp.full((TQ, LANES), -jnp.inf, F32)
        for c in range(i + 1):
            c0 = c * TQ
            sc = lax.dot_general(q, k_ref[c0:c0 + TQ, :], nt, preferred_element_type=F32)
            sc = sc + (fq - fk_all[:, c0:c0 + TQ])
            if c == i:
                r_id = lax.broadcasted_iota(jnp.int32, (TQ, TQ), 0)
                c_id = lax.broadcasted_iota(jnp.int32, (TQ, TQ), 1)
                sc = jnp.where(c_id <= r_id, sc, -jnp.inf)
            s_ref[:,c0:c0 + TQ] = sc
            for t in range(TQ // LANES):
                m_part = jnp.maximum(m_part, sc[:, t * LANES:(t + 1) * LANES])
        m = jnp.max(m_part, axis=1, keepdims=True)
        l_part = jnp.zeros((TQ, LANES), F32)
        for c in range(i + 1):
            c0 = c * TQ
            p = jnp.exp2(s_ref[:,c0:c0 + TQ] - m)
            for t in range(TQ // LANES):
                l_part = l_part + p[:, t * LANES:(t + 1) * LANES]
            p_ref[:,c0:c0 + TQ] = p.astype(BF16)
        l = jnp.sum(l_part, axis=1, keepdims=True)
        kv = q0 + TQ
        o = jnp.dot(p_ref[:,0:kv], v_ref[0:kv, :], preferred_element_type=F32)
        o_ref[q0:q0 + TQ, :] = (o / l).astype(BF16)


def _attention(proj, fcol, frow, b, s):
    t = b * s
    qb = N_LOCAL_COLS // HEAD_DIM
    return pl.pallas_call(
        _attn_kernel,
        grid=(b, N_HEADS),
        in_specs=[
            pl.BlockSpec((s, HEAD_DIM), lambda i, h: (i, qb + h)),
            pl.BlockSpec((s, HEAD_DIM), lambda i, h: (i, qb + N_HEADS + h)),
            pl.BlockSpec((s, HEAD_DIM), lambda i, h: (i, qb + 2 * N_HEADS + h)),
            pl.BlockSpec((s, LANES), lambda i, h: (i, 0)),
            pl.BlockSpec((None, 8, s), lambda i, h: (i, 0, 0)),
        ],
        out_specs=pl.BlockSpec((s, HEAD_DIM), lambda i, h: (i, h)),
        out_shape=jax.ShapeDtypeStruct((t, GROUP_W), BF16),
        scratch_shapes=[pltpu.VMEM((TQ, s), F32), pltpu.VMEM((TQ, s), BF16)],
        compiler_params=_params("parallel", "arbitrary"),
        name="fox_attention",
    )(proj, proj, proj, fcol, frow)


def _local_kernel(main_ref, halo_ref, cw_ref, lng_ref, lnb_ref, pw_ref, ps_ref, sw_ref,
                  o_ref, u_ref, p_ref, z_ref, pooled_ref):
    ts = main_ref.shape[0]
    i = pl.program_id(1)
    g = GROUP_W

    def rows(v):
        return jnp.concatenate([v] * (ROWS // SUBLANES), axis=0)

    def fill(dst0, src):
        src = src.astype(F32)
        a, gt = src[:, 0:g], src[:, g:2 * g]
        u_ref[0, dst0:dst0 + src.shape[0], :] = a * jax.nn.sigmoid(gt)
        p_ref[dst0:dst0 + src.shape[0], :] = src[:, 2 * g:3 * g]
        z_ref[dst0:dst0 + src.shape[0], :] = src[:, 4 * g:5 * g] * src[:, 5 * g:6 * g]

    halo = jnp.where(i > 0, halo_ref[...], jnp.zeros_like(halo_ref))
    fill(0, halo)
    for r in range(0, ts, ROWS):
        fill(HALO + r, main_ref[r:r + ROWS, :])
    n_ext = HALO + ts
    for sft in range(1, SUBLANES):
        for r in range(SUBLANES, n_ext, 4 * ROWS):
            r1 = min(r + 4 * ROWS, n_ext)
            u_ref[sft, r:r1, :] = u_ref[0, r - sft:r1 - sft, :]

    row_id = lax.broadcasted_iota(jnp.int32, (ROWS, POOL_GROUP), 0)
    for r in range(0, ts, ROWS):
        e0 = HALO + r
        acc = jnp.zeros((ROWS, g), F32)
        for k in range(CONF_K):
            back8, sft = divmod(CONF_K - 1 - k, SUBLANES)
            off = e0 - SUBLANES * back8
            acc = acc + rows(cw_ref[k]) * u_ref[sft, off:off + ROWS, :]
        mu = jnp.mean(acc, axis=-1, keepdims=True)
        cen = acc - mu
        var = jnp.mean(cen * cen, axis=-1, keepdims=True)
        y = cen * lax.rsqrt(var + EPS) * rows(lng_ref[...]) + rows(lnb_ref[...])
        o_ref[r:r + ROWS, 0:g] = (y * jax.nn.sigmoid(y)).astype(BF16)

        pos = i * ts + r + row_id
        for gi, w in enumerate(POOL_WINDOWS):
            c0 = gi * POOL_GROUP
            xt = p_ref[e0:e0 + ROWS, c0:c0 + POOL_GROUP]
            win = xt
            for jj in range(1, w):
                win = win + p_ref[e0 - jj:e0 - jj + ROWS, c0:c0 + POOL_GROUP]
            cnt = jnp.minimum(pos + 1, w).astype(F32)
            pooled_ref[r:r + ROWS, c0:c0 + POOL_GROUP] = (win / cnt - xt).astype(BF16)

        conv = jnp.zeros((ROWS, g), F32)
        for k in range(SC_K):
            off = e0 - (SC_K - 1) + k
            conv = conv + rows(sw_ref[k]) * z_ref[off:off + ROWS, :]
        bg = main_ref[r:r + ROWS, 3 * g:4 * g].astype(F32)
        o_ref[r:r + ROWS, 2 * g:3 * g] = (bg * conv).astype(BF16)

    for gi in range(len(POOL_WINDOWS)):
        c0 = gi * POOL_GROUP
        y = jnp.dot(pooled_ref[:, c0:c0 + POOL_GROUP], pw_ref[gi], preferred_element_type=F32)
        o_ref[:, g + c0:g + c0 + POOL_GROUP] = (y * ps_ref[:, c0:c0 + POOL_GROUP]).astype(BF16)


def _local_mixers(proj, conf_dw, ln_g, ln_b, pool_w, pool_scale, sc_dw, b, s):
    t = b * s
    ts = min(TS_LOCAL, s)
    nst = s // ts
    hb = ts // HALO
    g = GROUP_W

    def halo_idx(bi, si):
        return (jnp.maximum(bi * nst * hb + si * hb - 1, 0), 0)

    full = lambda shape: pl.BlockSpec(shape, lambda bi, si: (0,) * len(shape))
    return pl.pallas_call(
        _local_kernel,
        grid=(b, nst),
        in_specs=[
            pl.BlockSpec((ts, N_LOCAL_COLS), lambda bi, si: (bi * nst + si, 0)),
            pl.BlockSpec((HALO, N_LOCAL_COLS), halo_idx),
            full((CONF_K, SUBLANES, g)), full((SUBLANES, g)), full((SUBLANES, g)),
            full((len(POOL_WINDOWS), POOL_GROUP, POOL_GROUP)), full((1, g)), full((SC_K, SUBLANES, g)),
        ],
        out_specs=pl.BlockSpec((ts, 3 * g), lambda bi, si: (bi * nst + si, 0)),
        out_shape=jax.ShapeDtypeStruct((t, 3 * g), BF16),
        scratch_shapes=[pltpu.VMEM((SUBLANES, HALO + ts, g), F32)] + [pltpu.VMEM((HALO + ts, g), F32)] * 2
        + [pltpu.VMEM((ts, g), BF16)],
        compiler_params=_params("parallel", "parallel"),
        name="local_mixers",
    )(proj, proj, conf_dw, ln_g, ln_b, pool_w, pool_scale, sc_dw)


def _post_norm_residual(y, x, gain):
    d = y.shape[1]
    return x + y * _lanes(_inv_rms(y), d) * gain


def _out_proj_kernel(ya_ref, yl_ref, w_ref, x_ref, g_ref, xo_ref, ho_ref, rso_ref):
    g = GROUP_W
    for r0 in range(0, x_ref.shape[0], SUB):
        y = (jnp.dot(ya_ref[r0:r0 + SUB, :], w_ref[0:g, :], preferred_element_type=F32)
             + jnp.dot(yl_ref[r0:r0 + SUB, :], w_ref[g:, :], preferred_element_type=F32))
        x_new = _post_norm_residual(y, x_ref[r0:r0 + SUB, :], g_ref[...])
        _emit_stream(x_new, r0, xo_ref, ho_ref, rso_ref)


def _stream_out(t, d, tm, idx):
    specs = [pl.BlockSpec((tm, d), idx), pl.BlockSpec((tm, d), idx), pl.BlockSpec((tm, LANES), idx)]
    shapes = [jax.ShapeDtypeStruct((t, d), F32), jax.ShapeDtypeStruct((t, d), BF16),
              jax.ShapeDtypeStruct((t, LANES), F32)]
    return specs, shapes


def _out_proj(y_att, y_loc, w_out, x, gain):
    t, d = x.shape
    tm = min(TM_OUT, t)
    out_specs, out_shape = _stream_out(t, d, tm, lambda i: (i, 0))
    return pl.pallas_call(
        _out_proj_kernel,
        grid=(t // tm,),
        in_specs=[
            pl.BlockSpec((tm, GROUP_W), lambda i: (i, 0)),
            pl.BlockSpec((tm, 3 * GROUP_W), lambda i: (i, 0)),
            pl.BlockSpec((d, d), lambda i: (0, 0)),
            pl.BlockSpec((tm, d), lambda i: (i, 0)),
            pl.BlockSpec((1, d), lambda i: (0, 0)),
        ],
        out_specs=out_specs,
        out_shape=out_shape,
        compiler_params=_params("parallel"),
        name="out_proj",
    )(y_att, y_loc, w_out, x, gain)


def _mlp_kernel(x_ref, h_ref, rs_ref, w1_ref, w2_ref, g_ref, xo_ref, ho_ref, rso_ref):
    f = pl.program_id(1)

    @pl.when(f == 0)
    def _():
        xo_ref[...] = jnp.zeros_like(xo_ref)

    u = jnp.dot(h_ref[...], w1_ref[...], preferred_element_type=F32)
    u = jnp.square(jnp.maximum(u, 0.0)).astype(BF16)
    xo_ref[...] += jnp.dot(u, w2_ref[...], preferred_element_type=F32)

    @pl.when(f == pl.num_programs(1) - 1)
    def _():
        d = xo_ref.shape[1]
        for r0 in range(0, x_ref.shape[0], SUB):
            rs = rs_ref[r0:r0 + SUB, :]
            y = xo_ref[r0:r0 + SUB, :] * _lanes(rs * rs, d)
            x_new = _post_norm_residual(y, x_ref[r0:r0 + SUB, :], g_ref[...])
            _emit_stream(x_new, r0, xo_ref, ho_ref, rso_ref)


def _mlp(x, h, rs, w1, w2, gain):
    t, d = x.shape
    tf = TF_MLP
    tm = min(TM_MLP, t)
    out_specs, out_shape = _stream_out(t, d, tm, lambda i, f: (i, 0))
    return pl.pallas_call(
        _mlp_kernel,
        grid=(t // tm, w1.shape[1] // tf),
        in_specs=[
            pl.BlockSpec((tm, d), lambda i, f: (i, 0)),
            pl.BlockSpec((tm, d), lambda i, f: (i, 0)),
            pl.BlockSpec((tm, LANES), lambda i, f: (i, 0)),
            pl.BlockSpec((d, tf), lambda i, f: (0, f)),
            pl.BlockSpec((tf, d), lambda i, f: (f, 0)),
            pl.BlockSpec((1, d), lambda i, f: (0, 0)),
        ],
        out_specs=out_specs,
        out_shape=out_shape,
        compiler_params=_params("parallel", "arbitrary"),
        name="mlp",
    )(x, h, rs, w1, w2, gain)


def kernel(x, mix_norm_pre, w_in, b_forget, conf_dw, conf_ln_g, conf_ln_b, pool_w, pool_scale,
           sc_dw, w_out, mix_norm_post, mlp_norm_pre, w_mlp1, w_mlp2, mlp_norm_post):
    b, s, d = x.shape
    depth = w_in.shape[0]
    g = GROUP_W
    o_gate = 3 * g
    o_conf = o_gate + N_HEADS
    col_scale = jnp.where(jnp.arange(w_in.shape[-1]) < g, LOG2E / (HEAD_DIM ** 0.5), 1.0).astype(F32)
    w_in_g = (w_in * mix_norm_pre.astype(F32)[:, :, None] * col_scale).astype(BF16)
    w_main = jnp.concatenate([w_in_g[:, :, o_conf:], w_in_g[:, :, :o_gate]], axis=-1)
    w_gate = jnp.pad(w_in_g[:, :, o_gate:o_conf], ((0, 0), (0, 0), (0, LANES - N_HEADS)))
    b_gate = jnp.pad(b_forget.astype(F32), ((0, 0), (0, LANES - N_HEADS)))[:, None, :]
    w_out_b = w_out.astype(BF16)
    w1_b = (w_mlp1 * mlp_norm_pre.astype(F32)[:, :, None]).astype(BF16)
    w2_b = w_mlp2.astype(BF16)
    pool_w_b = pool_w.astype(BF16)
    row = lambda a: a.astype(F32)[:, None, :]
    rep = lambda a: jnp.broadcast_to(a.astype(F32)[..., None, :], a.shape[:-1] + (SUBLANES, a.shape[-1]))
    n_post, m_post = row(mix_norm_post), row(mlp_norm_post)
    ln_g, ln_b, p_scale = rep(conf_ln_g), rep(conf_ln_b), row(pool_scale)
    conf_dw, sc_dw = rep(conf_dw), rep(sc_dw)

    xt = x.reshape(b * s, d)
    stream = (xt,)
    for l in range(depth):
        proj, gate = _in_proj(stream, w_main[l], w_gate[l])
        fcol, frow = _gate_prefix(gate, b_gate[l], b, s)
        y_att = _attention(proj, fcol, frow, b, s)
        y_loc = _local_mixers(proj, conf_dw[l], ln_g[l], ln_b[l], pool_w_b[l], p_scale[l], sc_dw[l], b, s)
        xt, h, rs = _out_proj(y_att, y_loc, w_out_b[l], xt, n_post[l])
        xt, h, rs = _mlp(xt, h, rs, w1_b[l], w2_b[l], m_post[l])
        stream = (h, rs)
    return xt.reshape(b, s, d)
```

```python
import jax
import jax.numpy as jnp
from jax import lax
from jax.experimental import pallas as pl
from jax.experimental.pallas import tpu as pltpu

F32 = jnp.float32
BF16 = jnp.bfloat16

EPS = 1e-6
LOG2E = 1.4426950408889634
N_HEADS = 4
HEAD_DIM = 128
GROUP_W = N_HEADS * HEAD_DIM
CONF_K = 31
SC_K = 3
POOL_WINDOWS = (2, 4, 8, 16)
POOL_GROUP = GROUP_W // len(POOL_WINDOWS)
LANES = 128
SUBLANES = 8
HALO = 32
N_LOCAL_COLS = 6 * GROUP_W
N_MAIN_COLS = 9 * GROUP_W
VMEM_LIMIT = 56 * 1024 * 1024

TM_IN, TN_IN = 1024, 768
TM_OUT = 512
TM_MLP, TF_MLP = 512, 1024
SUB = 128
TS_LOCAL = 512
ROWS = 32
TQ = 256


def _params(*sem):
    return pltpu.CompilerParams(dimension_semantics=sem, vmem_limit_bytes=VMEM_LIMIT)


def _lanes(v, width):
    return jnp.concatenate([v] * (width // LANES), axis=1)


def _inv_rms(v):
    ms = jnp.mean(v * v, axis=-1, keepdims=True)
    return jnp.broadcast_to(lax.rsqrt(ms + EPS), (v.shape[0], LANES))


def _emit_stream(x_new, r0, x_ref, h_ref, rs_ref):
    rows = x_new.shape[0]
    x_ref[r0:r0 + rows, :] = x_new
    h_ref[r0:r0 + rows, :] = x_new.astype(BF16)
    rs_ref[r0:r0 + rows, :] = _inv_rms(x_new)


N_LOC_STEPS = N_LOCAL_COLS // TN_IN


def _in_proj_core(h_ref, rs_ref, wl_ref, wq_ref, wg_ref, proj_ref, gate_ref):
    j = pl.program_id(1)
    rs = rs_ref[...]

    @pl.when(j == 0)
    def _():
        gate_ref[...] = jnp.dot(h_ref[...], wg_ref[...], preferred_element_type=F32) * rs

    def project(w_ref):
        acc = jnp.dot(h_ref[...], w_ref[...], preferred_element_type=F32)
        proj_ref[...] = (acc * _lanes(rs, acc.shape[1])).astype(BF16)

    pl.when(j < N_LOC_STEPS)(lambda: project(wl_ref))
    pl.when(j >= N_LOC_STEPS)(lambda: project(wq_ref))


def _in_proj_kernel(h_ref, rs_ref, wl_ref, wq_ref, wg_ref, proj_ref, gate_ref):
    _in_proj_core(h_ref, rs_ref, wl_ref, wq_ref, wg_ref, proj_ref, gate_ref)


def _in_proj_f32_kernel(x_ref, wl_ref, wq_ref, wg_ref, proj_ref, gate_ref, h_ref, rs_ref):
    @pl.when(pl.program_id(1) == 0)
    def _():
        for r0 in range(0, x_ref.shape[0], SUB):
            x = x_ref[r0:r0 + SUB, :]
            h_ref[r0:r0 + SUB, :] = x.astype(BF16)
            rs_ref[r0:r0 + SUB, :] = _inv_rms(x)

    _in_proj_core(h_ref, rs_ref, wl_ref, wq_ref, wg_ref, proj_ref, gate_ref)


def _in_proj(stream, w_loc, w_qkv, w_gate, layer):
    t, d = stream[0].shape
    tm = min(TM_IN, t)
    n = w_loc.shape[2] + w_qkv.shape[2]
    first = len(stream) == 1
    stream_specs = [pl.BlockSpec((tm, d), lambda i, j: (i, 0))]
    if not first:
        stream_specs.append(pl.BlockSpec((tm, LANES), lambda i, j: (i, 0)))
    return pl.pallas_call(
        _in_proj_f32_kernel if first else _in_proj_kernel,
        grid=(t // tm, n // TN_IN),
        in_specs=stream_specs + [
            pl.BlockSpec((None, d, TN_IN), lambda i, j: (layer, 0, jnp.minimum(j, N_LOC_STEPS - 1))),
            pl.BlockSpec((None, d, TN_IN), lambda i, j: (layer, 0, jnp.maximum(j - N_LOC_STEPS, 0))),
            pl.BlockSpec((None, d, LANES), lambda i, j: (layer, 0, 0)),
        ],
        out_specs=[
            pl.BlockSpec((tm, TN_IN), lambda i, j: (i, j)),
            pl.BlockSpec((tm, LANES), lambda i, j: (i, 0)),
        ],
        out_shape=[
            jax.ShapeDtypeStruct((t, n), BF16),
            jax.ShapeDtypeStruct((t, LANES), F32),
        ],
        scratch_shapes=[pltpu.VMEM((tm, d), BF16), pltpu.VMEM((tm, LANES), F32)] if first else [],
        compiler_params=_params("parallel", "arbitrary"),
        name="in_proj",
    )(*stream, w_loc, w_qkv, w_gate)


def _split3(x):
    hi = x.astype(BF16)
    r1 = x - hi.astype(F32)
    mid = r1.astype(BF16)
    lo = (r1 - mid.astype(F32)).astype(BF16)
    return hi, mid, lo


def _gate_kernel(g_ref, b_ref, fcol_ref, frow_ref):
    s = g_ref.shape[0]
    row = lax.broadcasted_iota(jnp.int32, (LANES, LANES), 0)
    col = lax.broadcasted_iota(jnp.int32, (LANES, LANES), 1)
    tri = jnp.where(row >= col, 1.0, 0.0).astype(BF16)
    carry = jnp.zeros((1, LANES), F32)
    for j in range(s // LANES):
        z = g_ref[j * LANES:(j + 1) * LANES, :] + b_ref[...]
        logf = (jnp.minimum(z, 0.0) - jnp.log1p(jnp.exp(-jnp.abs(z)))) * LOG2E
        hi, mid, lo = _split3(logf)
        cum = (jnp.dot(tri, hi, preferred_element_type=F32)
               + jnp.dot(tri, mid, preferred_element_type=F32)
               + jnp.dot(tri, lo, preferred_element_type=F32)) + carry
        carry = cum[LANES - 1:LANES, :]
        fcol_ref[j * LANES:(j + 1) * LANES, :] = cum
        frow_ref[:, j * LANES:(j + 1) * LANES] = cum.T[0:8, :]


def _gate_prefix(gate, bias, b, s):
    return pl.pallas_call(
        _gate_kernel,
        grid=(b,),
        in_specs=[
            pl.BlockSpec((s, LANES), lambda i: (i, 0)),
            pl.BlockSpec((1, LANES), lambda i: (0, 0)),
        ],
        out_specs=[
            pl.BlockSpec((s, LANES), lambda i: (i, 0)),
            pl.BlockSpec((None, 8, s), lambda i: (i, 0, 0)),
        ],
        out_shape=[
            jax.ShapeDtypeStruct((b * s, LANES), F32),
            jax.ShapeDtypeStruct((b, 8, s), F32),
        ],
        compiler_params=_params("parallel"),
        name="gate_prefix",
    )(gate, bias)


def _attn_kernel(q_ref, k_ref, v_ref, fcol_ref, frow_ref, o_ref, s_ref, p_ref):
    s_len = q_ref.shape[0]
    h = pl.program_id(1)
    lane = lax.broadcasted_iota(jnp.int32, (s_len, LANES), 1)
    fq_all = jnp.sum(jnp.where(lane == h, fcol_ref[...], 0.0), axis=1, keepdims=True)
    fk_all = frow_ref[pl.ds(h, 1), :]
    nt = (((1,), (1,)), ((), ()))
    for i in range(s_len // TQ):
        q0 = i * TQ
        q = q_ref[q0:q0 + TQ, :]
        fq = fq_all[q0:q0 + TQ, :]
        m_part = jnp.full((TQ, LANES), -jnp.inf, F32)
        for c in range(i + 1):
            c0 = c * TQ
            sc = lax.dot_general(q, k_ref[c0:c0 + TQ, :], nt, preferred_element_type=F32)
            sc = sc + (fq - fk_all[:, c0:c0 + TQ])
            if c == i:
                r_id = lax.broadcasted_iota(jnp.int32, (TQ, TQ), 0)
                c_id = lax.broadcasted_iota(jnp.int32, (TQ, TQ), 1)
                sc = jnp.where(c_id <= r_id, sc, -jnp.inf)
            s_ref[:,c0:c0 + TQ] = sc
            for t in range(TQ // LANES):
                m_part = jnp.maximum(m_part, sc[:, t * LANES:(t + 1) * LANES])
        m = jnp.max(m_part, axis=1, keepdims=True)
        l_part = jnp.zeros((TQ, LANES), F32)
        for c in range(i + 1):
            c0 = c * TQ
            p = jnp.exp2(s_ref[:,c0:c0 + TQ] - m)
            for t in range(TQ // LANES):
                l_part = l_part + p[:, t * LANES:(t + 1) * LANES]
            p_ref[:,c0:c0 + TQ] = p.astype(BF16)
        l = jnp.sum(l_part, axis=1, keepdims=True)
        kv = q0 + TQ
        o = jnp.dot(p_ref[:,0:kv], v_ref[0:kv, :], preferred_element_type=F32)
        o_ref[q0:q0 + TQ, :] = (o / l).astype(BF16)


def _attention(proj, fcol, frow, b, s):
    t = b * s
    qb = N_LOCAL_COLS // HEAD_DIM
    return pl.pallas_call(
        _attn_kernel,
        grid=(b, N_HEADS),
        in_specs=[
            pl.BlockSpec((s, HEAD_DIM), lambda i, h: (i, qb + h)),
            pl.BlockSpec((s, HEAD_DIM), lambda i, h: (i, qb + N_HEADS + h)),
            pl.BlockSpec((s, HEAD_DIM), lambda i, h: (i, qb + 2 * N_HEADS + h)),
            pl.BlockSpec((s, LANES), lambda i, h: (i, 0)),
            pl.BlockSpec((None, 8, s), lambda i, h: (i, 0, 0)),
        ],
        out_specs=pl.BlockSpec((s, HEAD_DIM), lambda i, h: (i, h)),
        out_shape=jax.ShapeDtypeStruct((t, GROUP_W), BF16),
        scratch_shapes=[pltpu.VMEM((TQ, s), F32), pltpu.VMEM((TQ, s), BF16)],
        compiler_params=_params("parallel", "arbitrary"),
        name="fox_attention",
    )(proj, proj, proj, fcol, frow)


def _local_body(i, main_ref, halo_ref, cw_ref, lng_ref, lnb_ref, pw_ref, ps_ref, sw_ref,
                o_ref, u_ref, p_ref, z_ref, pooled_ref):
    ts = main_ref.shape[0]
    g = GROUP_W

    def rows(v):
        return jnp.concatenate([v] * (ROWS // SUBLANES), axis=0)

    def fill(dst0, src):
        src = src.astype(F32)
        a, gt = src[:, 0:g], src[:, g:2 * g]
        u_ref[0, dst0:dst0 + src.shape[0], :] = a * jax.nn.sigmoid(gt)
        p_ref[dst0:dst0 + src.shape[0], :] = src[:, 2 * g:3 * g]
        z_ref[dst0:dst0 + src.shape[0], :] = src[:, 4 * g:5 * g] * src[:, 5 * g:6 * g]

    halo = jnp.where(i > 0, halo_ref[...], jnp.zeros_like(halo_ref))
    fill(0, halo)
    for r in range(0, ts, ROWS):
        fill(HALO + r, main_ref[r:r + ROWS, :])
    n_ext = HALO + ts
    for sft in range(1, SUBLANES):
        for r in range(SUBLANES, n_ext, 4 * ROWS):
            r1 = min(r + 4 * ROWS, n_ext)
            u_ref[sft, r:r1, :] = u_ref[0, r - sft:r1 - sft, :]

    row_id = lax.broadcasted_iota(jnp.int32, (ROWS, POOL_GROUP), 0)
    for r in range(0, ts, ROWS):
        e0 = HALO + r
        acc = jnp.zeros((ROWS, g), F32)
        for k in range(CONF_K):
            back8, sft = divmod(CONF_K - 1 - k, SUBLANES)
            off = e0 - SUBLANES * back8
            acc = acc + rows(cw_ref[k]) * u_ref[sft, off:off + ROWS, :]
        mu = jnp.mean(acc, axis=-1, keepdims=True)
        cen = acc - mu
        var = jnp.mean(cen * cen, axis=-1, keepdims=True)
        y = cen * lax.rsqrt(var + EPS) * rows(lng_ref[...]) + rows(lnb_ref[...])
        o_ref[r:r + ROWS, 0:g] = (y * jax.nn.sigmoid(y)).astype(BF16)

        pos = i * ts + r + row_id
        for gi, w in enumerate(POOL_WINDOWS):
            c0 = gi * POOL_GROUP
            xt = p_ref[e0:e0 + ROWS, c0:c0 + POOL_GROUP]
            win = xt
            for jj in range(1, w):
                win = win + p_ref[e0 - jj:e0 - jj + ROWS, c0:c0 + POOL_GROUP]
            cnt = jnp.minimum(pos + 1, w).astype(F32)
            pooled_ref[r:r + ROWS, c0:c0 + POOL_GROUP] = (win / cnt - xt).astype(BF16)

        conv = jnp.zeros((ROWS, g), F32)
        for k in range(SC_K):
            off = e0 - (SC_K - 1) + k
            conv = conv + rows(sw_ref[k]) * z_ref[off:off + ROWS, :]
        bg = main_ref[r:r + ROWS, 3 * g:4 * g].astype(F32)
        o_ref[r:r + ROWS, 2 * g:3 * g] = (bg * conv).astype(BF16)

    for gi in range(len(POOL_WINDOWS)):
        c0 = gi * POOL_GROUP
        y = jnp.dot(pooled_ref[:, c0:c0 + POOL_GROUP], pw_ref[gi], preferred_element_type=F32)
        o_ref[:, g + c0:g + c0 + POOL_GROUP] = (y * ps_ref[:, c0:c0 + POOL_GROUP]).astype(BF16)


def _local_kernel(*refs):
    _local_body(pl.program_id(1), *refs)


def _local_mixers(proj, conf_dw, ln_g, ln_b, pool_w, pool_scale, sc_dw, b, s):
    t = b * s
    ts = min(TS_LOCAL, s)
    nst = s // ts
    hb = ts // HALO
    g = GROUP_W

    def halo_idx(bi, si):
        return (jnp.maximum(bi * nst * hb + si * hb - 1, 0), 0)

    full = lambda shape: pl.BlockSpec(shape, lambda bi, si: (0,) * len(shape))
    return pl.pallas_call(
        _local_kernel,
        grid=(b, nst),
        in_specs=[
            pl.BlockSpec((ts, N_LOCAL_COLS), lambda bi, si: (bi * nst + si, 0)),
            pl.BlockSpec((HALO, N_LOCAL_COLS), halo_idx),
            full((CONF_K, SUBLANES, g)), full((SUBLANES, g)), full((SUBLANES, g)),
            full((len(POOL_WINDOWS), POOL_GROUP, POOL_GROUP)), full((1, g)), full((SC_K, SUBLANES, g)),
        ],
        out_specs=pl.BlockSpec((ts, 3 * g), lambda bi, si: (bi * nst + si, 0)),
        out_shape=jax.ShapeDtypeStruct((t, 3 * g), BF16),
        scratch_shapes=[pltpu.VMEM((SUBLANES, HALO + ts, g), F32)] + [pltpu.VMEM((HALO + ts, g), F32)] * 2
        + [pltpu.VMEM((ts, g), BF16)],
        compiler_params=_params("parallel", "parallel"),
        name="local_mixers",
    )(proj, proj, conf_dw, ln_g, ln_b, pool_w, pool_scale, sc_dw)


def _post_norm_residual(y, x, gain):
    d = y.shape[1]
    return x + y * _lanes(_inv_rms(y), d) * gain


def _stream_out(t, d, tm, idx):
    specs = [pl.BlockSpec((tm, d), idx), pl.BlockSpec((tm, d), idx), pl.BlockSpec((tm, LANES), idx)]
    shapes = [jax.ShapeDtypeStruct((t, d), F32), jax.ShapeDtypeStruct((t, d), BF16),
              jax.ShapeDtypeStruct((t, LANES), F32)]
    return specs, shapes


def _out_proj_kernel(ya_ref, yl_ref, w_ref, x_ref, g_ref, xo_ref, ho_ref, rso_ref):
    g = GROUP_W
    for r0 in range(0, x_ref.shape[0], SUB):
        y = (jnp.dot(ya_ref[r0:r0 + SUB, :], w_ref[0:g, :], preferred_element_type=F32)
             + jnp.dot(yl_ref[r0:r0 + SUB, :], w_ref[g:, :], preferred_element_type=F32))
        x_new = _post_norm_residual(y, x_ref[r0:r0 + SUB, :], g_ref[...])
        _emit_stream(x_new, r0, xo_ref, ho_ref, rso_ref)


def _out_proj(y_att, y_loc, w_out, layer, x, gain):
    t, d = x.shape
    tm = min(TM_OUT, t)
    out_specs, out_shape = _stream_out(t, d, tm, lambda i: (i, 0))
    return pl.pallas_call(
        _out_proj_kernel,
        grid=(t // tm,),
        in_specs=[
            pl.BlockSpec((tm, GROUP_W), lambda i: (i, 0)),
            pl.BlockSpec((tm, 3 * GROUP_W), lambda i: (i, 0)),
            pl.BlockSpec((None, d, d), lambda i: (layer, 0, 0)),
            pl.BlockSpec((tm, d), lambda i: (i, 0)),
            pl.BlockSpec((1, d), lambda i: (0, 0)),
        ],
        out_specs=out_specs,
        out_shape=out_shape,
        compiler_params=_params("parallel"),
        name="out_proj",
    )(y_att, y_loc, w_out, x, gain)


def _mlp_kernel(x_ref, h_ref, rs_ref, w1_ref, w2_ref, g_ref, xo_ref, ho_ref, rso_ref):
    f = pl.program_id(1)

    @pl.when(f == 0)
    def _():
        xo_ref[...] = jnp.zeros_like(xo_ref)

    u = jnp.dot(h_ref[...], w1_ref[...], preferred_element_type=F32)
    u = jnp.square(jnp.maximum(u, 0.0)).astype(BF16)
    xo_ref[...] += jnp.dot(u, w2_ref[...], preferred_element_type=F32)

    @pl.when(f == pl.num_programs(1) - 1)
    def _():
        d = xo_ref.shape[1]
        for r0 in range(0, x_ref.shape[0], SUB):
            rs = rs_ref[r0:r0 + SUB, :]
            y = xo_ref[r0:r0 + SUB, :] * _lanes(rs * rs, d)
            x_new = _post_norm_residual(y, x_ref[r0:r0 + SUB, :], g_ref[...])
            _emit_stream(x_new, r0, xo_ref, ho_ref, rso_ref)


def _mlp(x, h, rs, w1, w2, layer, gain):
    t, d = x.shape
    tf = TF_MLP
    tm = min(TM_MLP, t)
    out_specs, out_shape = _stream_out(t, d, tm, lambda i, f: (i, 0))
    return pl.pallas_call(
        _mlp_kernel,
        grid=(t // tm, w1.shape[2] // tf),
        in_specs=[
            pl.BlockSpec((tm, d), lambda i, f: (i, 0)),
            pl.BlockSpec((tm, d), lambda i, f: (i, 0)),
            pl.BlockSpec((tm, LANES), lambda i, f: (i, 0)),
            pl.BlockSpec((None, d, tf), lambda i, f: (layer, 0, f)),
            pl.BlockSpec((None, tf, d), lambda i, f: (layer, f, 0)),
            pl.BlockSpec((1, d), lambda i, f: (0, 0)),
        ],
        out_specs=out_specs,
        out_shape=out_shape,
        compiler_params=_params("parallel", "arbitrary"),
        name="mlp",
    )(x, h, rs, w1, w2, gain)


def kernel(x, mix_norm_pre, w_in, b_forget, conf_dw, conf_ln_g, conf_ln_b, pool_w, pool_scale,
           sc_dw, w_out, mix_norm_post, mlp_norm_pre, w_mlp1, w_mlp2, mlp_norm_post):
    b, s, d = x.shape
    depth = w_in.shape[0]
    g = GROUP_W
    o_gate = 3 * g
    o_conf = o_gate + N_HEADS
    col_scale = jnp.where(jnp.arange(o_gate) < g, LOG2E / (HEAD_DIM ** 0.5), 1.0).astype(F32)
    gain_in = mix_norm_pre.astype(F32)[:, :, None]
    w_loc = (w_in[:, :, o_conf:] * gain_in).astype(BF16)
    w_qkv = (w_in[:, :, :o_gate] * gain_in * col_scale).astype(BF16)
    w_gate = jnp.pad((w_in[:, :, o_gate:o_conf] * gain_in).astype(BF16), ((0, 0), (0, 0), (0, LANES - N_HEADS)))
    b_gate = jnp.pad(b_forget.astype(F32), ((0, 0), (0, LANES - N_HEADS)))[:, None, :]
    w_out_b = w_out.astype(BF16)
    w1_b = (w_mlp1 * mlp_norm_pre.astype(F32)[:, :, None]).astype(BF16)
    w2_b = w_mlp2.astype(BF16)
    pool_w_b = pool_w.astype(BF16)
    row = lambda a: a.astype(F32)[:, None, :]
    rep = lambda a: jnp.broadcast_to(a.astype(F32)[..., None, :], a.shape[:-1] + (SUBLANES, a.shape[-1]))
    n_post, m_post = row(mix_norm_post), row(mlp_norm_post)
    ln_g, ln_b, p_scale = rep(conf_ln_g), rep(conf_ln_b), row(pool_scale)
    conf_dw, sc_dw = rep(conf_dw), rep(sc_dw)

    xt = x.reshape(b * s, d)
    stream = (xt,)
    for l in range(depth):
        proj, gate = _in_proj(stream, w_loc, w_qkv, w_gate, l)
        fcol, frow = _gate_prefix(gate, b_gate[l], b, s)
        y_att = _attention(proj, fcol, frow, b, s)
        y_loc = _local_mixers(proj, conf_dw[l], ln_g[l], ln_b[l], pool_w_b[l], p_scale[l], sc_dw[l], b, s)
        xt, h, rs = _out_proj(y_att, y_loc, w_out_b, l, xt, n_post[l])
        xt, h, rs = _mlp(xt, h, rs, w1_b, w2_b, l, m_post[l])
        stream = (h, rs)
    return xt.reshape(b, s, d)
```

```python
import jax
import jax.numpy as jnp
from jax import lax
from jax.experimental import pallas as pl
from jax.experimental.pallas import tpu as pltpu

F32 = jnp.float32
BF16 = jnp.bfloat16

EPS = 1e-6
LOG2E = 1.4426950408889634
N_HEADS = 4
HEAD_DIM = 128
GROUP_W = N_HEADS * HEAD_DIM
CONF_K = 31
SC_K = 3
POOL_WINDOWS = (2, 4, 8, 16)
POOL_GROUP = GROUP_W // len(POOL_WINDOWS)
LANES = 128
SUBLANES = 8
HALO = 32
N_LOCAL_COLS = 6 * GROUP_W
VMEM_LIMIT = 56 * 1024 * 1024

TM_IN, TN_IN = 1024, 768
TM_OUT = 512
TM_MLP, TF_MLP = 512, 1024
SUB = 128
TS_LOCAL = 512
ROWS = 32
TQ = 256


def _params(*sem):
    return pltpu.CompilerParams(dimension_semantics=sem, vmem_limit_bytes=VMEM_LIMIT)


def _lanes(v, width):
    return jnp.concatenate([v] * (width // LANES), axis=1)


def _inv_rms(v):
    ms = jnp.mean(v * v, axis=-1, keepdims=True)
    return jnp.broadcast_to(lax.rsqrt(ms + EPS), (v.shape[0], LANES))


def _emit_stream(x_new, r0, x_ref, h_ref, rs_ref):
    rows = x_new.shape[0]
    x_ref[r0:r0 + rows, :] = x_new
    h_ref[r0:r0 + rows, :] = x_new.astype(BF16)
    rs_ref[r0:r0 + rows, :] = _inv_rms(x_new)


def _project(h_ref, rs_ref, w_ref, proj_ref):
    acc = jnp.dot(h_ref[...], w_ref[...], preferred_element_type=F32)
    proj_ref[...] = (acc * _lanes(rs_ref[...], acc.shape[1])).astype(BF16)


def _in_proj_kernel(h_ref, rs_ref, w_ref, proj_ref):
    _project(h_ref, rs_ref, w_ref, proj_ref)


def _in_proj_gate_kernel(h_ref, rs_ref, w_ref, wg_ref, proj_ref, gate_ref):
    @pl.when(pl.program_id(1) == 0)
    def _():
        gate_ref[...] = jnp.dot(h_ref[...], wg_ref[...], preferred_element_type=F32) * rs_ref[...]

    _project(h_ref, rs_ref, w_ref, proj_ref)


def _in_proj_f32_kernel(x_ref, w_ref, proj_ref, h_ref, rs_ref):
    @pl.when(pl.program_id(1) == 0)
    def _():
        for r0 in range(0, x_ref.shape[0], SUB):
            x = x_ref[r0:r0 + SUB, :]
            h_ref[r0:r0 + SUB, :] = x.astype(BF16)
            rs_ref[r0:r0 + SUB, :] = _inv_rms(x)

    _project(h_ref, rs_ref, w_ref, proj_ref)


def _in_proj(stream, w, layer, w_gate=None):
    t, d = stream[0].shape
    tm = min(TM_IN, t)
    n = w.shape[2]
    first = len(stream) == 1
    row_tile = lambda width: pl.BlockSpec((tm, width), lambda i, j: (i, 0))
    in_specs = [row_tile(d)] + ([] if first else [row_tile(LANES)])
    in_specs.append(pl.BlockSpec((None, d, TN_IN), lambda i, j: (layer, 0, j)))
    out_specs = [pl.BlockSpec((tm, TN_IN), lambda i, j: (i, j))]
    out_shape = [jax.ShapeDtypeStruct((t, n), BF16)]
    operands = list(stream) + [w]
    if first:
        body = _in_proj_f32_kernel
        out_specs += [row_tile(d), row_tile(LANES)]
        out_shape += [jax.ShapeDtypeStruct((t, d), BF16), jax.ShapeDtypeStruct((t, LANES), F32)]
    elif w_gate is not None:
        body = _in_proj_gate_kernel
        in_specs.append(pl.BlockSpec((None, d, LANES), lambda i, j: (layer, 0, 0)))
        out_specs.append(row_tile(LANES))
        out_shape.append(jax.ShapeDtypeStruct((t, LANES), F32))
        operands.append(w_gate)
    else:
        body = _in_proj_kernel
    return pl.pallas_call(
        body,
        grid=(t // tm, n // TN_IN),
        in_specs=in_specs,
        out_specs=out_specs,
        out_shape=out_shape,
        compiler_params=_params("parallel", "arbitrary"),
        name="in_proj",
    )(*operands)


def _split3(x):
    hi = x.astype(BF16)
    r1 = x - hi.astype(F32)
    mid = r1.astype(BF16)
    lo = (r1 - mid.astype(F32)).astype(BF16)
    return hi, mid, lo


def _gate_kernel(g_ref, b_ref, fcol_ref, frow_ref):
    s = g_ref.shape[0]
    row = lax.broadcasted_iota(jnp.int32, (LANES, LANES), 0)
    col = lax.broadcasted_iota(jnp.int32, (LANES, LANES), 1)
    tri = jnp.where(row >= col, 1.0, 0.0).astype(BF16)
    carry = jnp.zeros((1, LANES), F32)
    for j in range(s // LANES):
        z = g_ref[j * LANES:(j + 1) * LANES, :] + b_ref[...]
        logf = (jnp.minimum(z, 0.0) - jnp.log1p(jnp.exp(-jnp.abs(z)))) * LOG2E
        hi, mid, lo = _split3(logf)
        cum = (jnp.dot(tri, hi, preferred_element_type=F32)
               + jnp.dot(tri, mid, preferred_element_type=F32)
               + jnp.dot(tri, lo, preferred_element_type=F32)) + carry
        carry = cum[LANES - 1:LANES, :]
        fcol_ref[j * LANES:(j + 1) * LANES, :] = cum
        frow_ref[:, j * LANES:(j + 1) * LANES] = cum.T[0:8, :]


def _gate_prefix(gate, bias, b, s):
    return pl.pallas_call(
        _gate_kernel,
        grid=(b,),
        in_specs=[
            pl.BlockSpec((s, LANES), lambda i: (i, 0)),
            pl.BlockSpec((1, LANES), lambda i: (0, 0)),
        ],
        out_specs=[
            pl.BlockSpec((s, LANES), lambda i: (i, 0)),
            pl.BlockSpec((None, 8, s), lambda i: (i, 0, 0)),
        ],
        out_shape=[
            jax.ShapeDtypeStruct((b * s, LANES), F32),
            jax.ShapeDtypeStruct((b, 8, s), F32),
        ],
        compiler_params=_params("parallel"),
        name="gate_prefix",
    )(gate, bias)


def _attn_kernel(q_ref, k_ref, v_ref, fcol_ref, frow_ref, o_ref, s_ref, p_ref):
    s_len = q_ref.shape[0]
    h = pl.program_id(1)
    lane = lax.broadcasted_iota(jnp.int32, (s_len, LANES), 1)
    fq_all = jnp.sum(jnp.where(lane == h, fcol_ref[...], 0.0), axis=1, keepdims=True)
    fk_all = frow_ref[pl.ds(h, 1), :]
    nt = (((1,), (1,)), ((), ()))
    n_q = s_len // TQ

    def score_chunk(i, c, m_part):
        q0, c0 = i * TQ, c * TQ
        sc = lax.dot_general(q_ref[q0:q0 + TQ, :], k_ref[c0:c0 + TQ, :], nt, preferred_element_type=F32)
        sc = sc + (fq_all[q0:q0 + TQ, :] - fk_all[:, c0:c0 + TQ])
        if c == i:
            r_id = lax.broadcasted_iota(jnp.int32, (TQ, TQ), 0)
            c_id = lax.broadcasted_iota(jnp.int32, (TQ, TQ), 1)
            sc = jnp.where(c_id <= r_id, sc, -jnp.inf)
        s_ref[i % 2, :, c0:c0 + TQ] = sc
        for t in range(TQ // LANES):
            m_part = jnp.maximum(m_part, sc[:, t * LANES:(t + 1) * LANES])
        return m_part

    def weight_chunk(i, c, m, l_part):
        c0 = c * TQ
        p = jnp.exp2(s_ref[i % 2, :, c0:c0 + TQ] - m)
        for t in range(TQ // LANES):
            l_part = l_part + p[:, t * LANES:(t + 1) * LANES]
        p_ref[i % 2, :, c0:c0 + TQ] = p.astype(BF16)
        return l_part

    neg_inf = jnp.full((TQ, LANES), -jnp.inf, F32)
    m_cur = jnp.max(score_chunk(0, 0, neg_inf), axis=1, keepdims=True)
    for i in range(n_q):
        l_part = jnp.zeros((TQ, LANES), F32)
        m_part = neg_inf
        for c in range(i + 2):
            if i + 1 < n_q:
                m_part = score_chunk(i + 1, c, m_part)
            if c <= i:
                l_part = weight_chunk(i, c, m_cur, l_part)
        l = jnp.sum(l_part, axis=1, keepdims=True)
        kv = (i + 1) * TQ
        o = jnp.dot(p_ref[i % 2, :, 0:kv], v_ref[0:kv, :], preferred_element_type=F32)
        o_ref[i * TQ:kv, :] = (o / l).astype(BF16)
        m_cur = jnp.max(m_part, axis=1, keepdims=True)


def _attention(qkv, fcol, frow, b, s):
    t = b * s
    return pl.pallas_call(
        _attn_kernel,
        grid=(b, N_HEADS),
        in_specs=[
            pl.BlockSpec((s, HEAD_DIM), lambda i, h: (i, h)),
            pl.BlockSpec((s, HEAD_DIM), lambda i, h: (i, N_HEADS + h)),
            pl.BlockSpec((s, HEAD_DIM), lambda i, h: (i, 2 * N_HEADS + h)),
            pl.BlockSpec((s, LANES), lambda i, h: (i, 0)),
            pl.BlockSpec((None, 8, s), lambda i, h: (i, 0, 0)),
        ],
        out_specs=pl.BlockSpec((s, HEAD_DIM), lambda i, h: (i, h)),
        out_shape=jax.ShapeDtypeStruct((t, GROUP_W), BF16),
        scratch_shapes=[pltpu.VMEM((2, TQ, s), F32), pltpu.VMEM((2, TQ, s), BF16)],
        compiler_params=_params("parallel", "arbitrary"),
        name="fox_attention",
    )(qkv, qkv, qkv, fcol, frow)


def _local_body(i, main_ref, halo_ref, cw_ref, lng_ref, lnb_ref, pw_ref, ps_ref, sw_ref,
                o_ref, u_ref, p_ref, z_ref, pooled_ref):
    ts = main_ref.shape[0]
    g = GROUP_W

    def rows(v):
        return jnp.concatenate([v] * (ROWS // SUBLANES), axis=0)

    def fill(dst0, src):
        src = src.astype(F32)
        a, gt = src[:, 0:g], src[:, g:2 * g]
        u_ref[0, dst0:dst0 + src.shape[0], :] = a * jax.nn.sigmoid(gt)
        p_ref[dst0:dst0 + src.shape[0], :] = src[:, 2 * g:3 * g]
        z_ref[dst0:dst0 + src.shape[0], :] = src[:, 4 * g:5 * g] * src[:, 5 * g:6 * g]

    halo = jnp.where(i > 0, halo_ref[...], jnp.zeros_like(halo_ref))
    fill(0, halo)
    for r in range(0, ts, ROWS):
        fill(HALO + r, main_ref[r:r + ROWS, :])
    n_ext = HALO + ts
    for sft in range(1, SUBLANES):
        for r in range(SUBLANES, n_ext, 4 * ROWS):
            r1 = min(r + 4 * ROWS, n_ext)
            u_ref[sft, r:r1, :] = u_ref[0, r - sft:r1 - sft, :]

    row_id = lax.broadcasted_iota(jnp.int32, (ROWS, POOL_GROUP), 0)
    for r in range(0, ts, ROWS):
        e0 = HALO + r
        acc = jnp.zeros((ROWS, g), F32)
        for k in range(CONF_K):
            back8, sft = divmod(CONF_K - 1 - k, SUBLANES)
            off = e0 - SUBLANES * back8
            acc = acc + rows(cw_ref[k]) * u_ref[sft, off:off + ROWS, :]
        mu = jnp.mean(acc, axis=-1, keepdims=True)
        cen = acc - mu
        var = jnp.mean(cen * cen, axis=-1, keepdims=True)
        y = cen * lax.rsqrt(var + EPS) * rows(lng_ref[...]) + rows(lnb_ref[...])
        o_ref[r:r + ROWS, 0:g] = (y * jax.nn.sigmoid(y)).astype(BF16)

        pos = i * ts + r + row_id
        for gi, w in enumerate(POOL_WINDOWS):
            c0 = gi * POOL_GROUP
            xt = p_ref[e0:e0 + ROWS, c0:c0 + POOL_GROUP]
            win = xt
            for jj in range(1, w):
                win = win + p_ref[e0 - jj:e0 - jj + ROWS, c0:c0 + POOL_GROUP]
            cnt = jnp.minimum(pos + 1, w).astype(F32)
            pooled_ref[r:r + ROWS, c0:c0 + POOL_GROUP] = (win / cnt - xt).astype(BF16)

        conv = jnp.zeros((ROWS, g), F32)
        for k in range(SC_K):
            off = e0 - (SC_K - 1) + k
            conv = conv + rows(sw_ref[k]) * z_ref[off:off + ROWS, :]
        bg = main_ref[r:r + ROWS, 3 * g:4 * g].astype(F32)
        o_ref[r:r + ROWS, 2 * g:3 * g] = (bg * conv).astype(BF16)

    for gi in range(len(POOL_WINDOWS)):
        c0 = gi * POOL_GROUP
        y = jnp.dot(pooled_ref[:, c0:c0 + POOL_GROUP], pw_ref[gi], preferred_element_type=F32)
        o_ref[:, g + c0:g + c0 + POOL_GROUP] = (y * ps_ref[:, c0:c0 + POOL_GROUP]).astype(BF16)


def _local_kernel(*refs):
    _local_body(pl.program_id(1), *refs)


def _local_mixers(proj, conf_dw, ln_g, ln_b, pool_w, pool_scale, sc_dw, b, s):
    t = b * s
    ts = min(TS_LOCAL, s)
    nst = s // ts
    hb = ts // HALO
    g = GROUP_W

    def halo_idx(bi, si):
        return (jnp.maximum(bi * nst * hb + si * hb - 1, 0), 0)

    full = lambda shape: pl.BlockSpec(shape, lambda bi, si: (0,) * len(shape))
    return pl.pallas_call(
        _local_kernel,
        grid=(b, nst),
        in_specs=[
            pl.BlockSpec((ts, N_LOCAL_COLS), lambda bi, si: (bi * nst + si, 0)),
            pl.BlockSpec((HALO, N_LOCAL_COLS), halo_idx),
            full((CONF_K, SUBLANES, g)), full((SUBLANES, g)), full((SUBLANES, g)),
            full((len(POOL_WINDOWS), POOL_GROUP, POOL_GROUP)), full((1, g)), full((SC_K, SUBLANES, g)),
        ],
        out_specs=pl.BlockSpec((ts, 3 * g), lambda bi, si: (bi * nst + si, 0)),
        out_shape=jax.ShapeDtypeStruct((t, 3 * g), BF16),
        scratch_shapes=[pltpu.VMEM((SUBLANES, HALO + ts, g), F32)] + [pltpu.VMEM((HALO + ts, g), F32)] * 2
        + [pltpu.VMEM((ts, g), BF16)],
        compiler_params=_params("parallel", "parallel"),
        name="local_mixers",
    )(proj, proj, conf_dw, ln_g, ln_b, pool_w, pool_scale, sc_dw)


def _post_norm_residual(y, x, gain):
    d = y.shape[1]
    return x + y * _lanes(_inv_rms(y), d) * gain


def _stream_out(t, d, tm, idx):
    specs = [pl.BlockSpec((tm, d), idx), pl.BlockSpec((tm, d), idx), pl.BlockSpec((tm, LANES), idx)]
    shapes = [jax.ShapeDtypeStruct((t, d), F32), jax.ShapeDtypeStruct((t, d), BF16),
              jax.ShapeDtypeStruct((t, LANES), F32)]
    return specs, shapes


def _out_proj_kernel(ya_ref, yl_ref, w_ref, x_ref, g_ref, xo_ref, ho_ref, rso_ref):
    g = GROUP_W
    for r0 in range(0, x_ref.shape[0], SUB):
        y = (jnp.dot(ya_ref[r0:r0 + SUB, :], w_ref[0:g, :], preferred_element_type=F32)
             + jnp.dot(yl_ref[r0:r0 + SUB, :], w_ref[g:, :], preferred_element_type=F32))
        x_new = _post_norm_residual(y, x_ref[r0:r0 + SUB, :], g_ref[...])
        _emit_stream(x_new, r0, xo_ref, ho_ref, rso_ref)


def _out_proj(y_att, y_loc, w_out, layer, x, gain):
    t, d = x.shape
    tm = min(TM_OUT, t)
    out_specs, out_shape = _stream_out(t, d, tm, lambda i: (i, 0))
    return pl.pallas_call(
        _out_proj_kernel,
        grid=(t // tm,),
        in_specs=[
            pl.BlockSpec((tm, GROUP_W), lambda i: (i, 0)),
            pl.BlockSpec((tm, 3 * GROUP_W), lambda i: (i, 0)),
            pl.BlockSpec((None, d, d), lambda i: (layer, 0, 0)),
            pl.BlockSpec((tm, d), lambda i: (i, 0)),
            pl.BlockSpec((1, d), lambda i: (0, 0)),
        ],
        out_specs=out_specs,
        out_shape=out_shape,
        compiler_params=_params("parallel"),
        name="out_proj",
    )(y_att, y_loc, w_out, x, gain)


def _mlp_kernel(x_ref, h_ref, rs_ref, w1_ref, w2_ref, g_ref, xo_ref, ho_ref, rso_ref):
    f = pl.program_id(1)
    last = pl.num_programs(1) - 1

    def hidden():
        u = jnp.dot(h_ref[...], w1_ref[...], preferred_element_type=F32)
        return jnp.square(jnp.maximum(u, 0.0)).astype(BF16)

    @pl.when(f == 0)
    def _():
        xo_ref[...] = jnp.dot(hidden(), w2_ref[...], preferred_element_type=F32)

    @pl.when((f > 0) & (f < last))
    def _():
        xo_ref[...] += jnp.dot(hidden(), w2_ref[...], preferred_element_type=F32)

    @pl.when(f == last)
    def _():
        d = xo_ref.shape[1]
        u = hidden()
        for r0 in range(0, x_ref.shape[0], SUB):
            r2 = rs_ref[r0:r0 + SUB, :] * rs_ref[r0:r0 + SUB, :]
            acc = xo_ref[r0:r0 + SUB, :] + jnp.dot(u[r0:r0 + SUB, :], w2_ref[...], preferred_element_type=F32)
            ms = jnp.broadcast_to(jnp.mean(acc * acc, axis=-1, keepdims=True), (SUB, LANES))
            scale = r2 * lax.rsqrt(r2 * r2 * ms + EPS)
            x_new = x_ref[r0:r0 + SUB, :] + acc * _lanes(scale, d) * g_ref[...]
            _emit_stream(x_new, r0, xo_ref, ho_ref, rso_ref)


def _mlp(x, h, rs, w1, w2, layer, gain):
    t, d = x.shape
    tf = TF_MLP
    assert w1.shape[2] // tf >= 2, "the MLP kernel needs separate first and last d_ff chunks"
    tm = min(TM_MLP, t)
    out_specs, out_shape = _stream_out(t, d, tm, lambda i, f: (i, 0))
    return pl.pallas_call(
        _mlp_kernel,
        grid=(t // tm, w1.shape[2] // tf),
        in_specs=[
            pl.BlockSpec((tm, d), lambda i, f: (i, 0)),
            pl.BlockSpec((tm, d), lambda i, f: (i, 0)),
            pl.BlockSpec((tm, LANES), lambda i, f: (i, 0)),
            pl.BlockSpec((None, d, tf), lambda i, f: (layer, 0, f)),
            pl.BlockSpec((None, tf, d), lambda i, f: (layer, f, 0)),
            pl.BlockSpec((1, d), lambda i, f: (0, 0)),
        ],
        out_specs=out_specs,
        out_shape=out_shape,
        compiler_params=_params("parallel", "arbitrary"),
        name="mlp",
    )(x, h, rs, w1, w2, gain)


def kernel(x, mix_norm_pre, w_in, b_forget, conf_dw, conf_ln_g, conf_ln_b, pool_w, pool_scale,
           sc_dw, w_out, mix_norm_post, mlp_norm_pre, w_mlp1, w_mlp2, mlp_norm_post):
    b, s, d = x.shape
    depth = w_in.shape[0]
    g = GROUP_W
    o_gate = 3 * g
    o_conf = o_gate + N_HEADS
    col_scale = jnp.where(jnp.arange(o_gate) < g, LOG2E / (HEAD_DIM ** 0.5), 1.0).astype(F32)
    gain_in = mix_norm_pre.astype(F32)[:, :, None]
    w_loc = (w_in[:, :, o_conf:] * gain_in).astype(BF16)
    w_qkv = (w_in[:, :, :o_gate] * gain_in * col_scale).astype(BF16)
    w_gate = jnp.pad((w_in[:, :, o_gate:o_conf] * gain_in).astype(BF16), ((0, 0), (0, 0), (0, LANES - N_HEADS)))
    b_gate = jnp.pad(b_forget.astype(F32), ((0, 0), (0, LANES - N_HEADS)))[:, None, :]
    w_out_b = w_out.astype(BF16)
    w1_b = (w_mlp1 * mlp_norm_pre.astype(F32)[:, :, None]).astype(BF16)
    w2_b = w_mlp2.astype(BF16)
    pool_w_b = pool_w.astype(BF16)
    row = lambda a: a.astype(F32)[:, None, :]
    rep = lambda a: jnp.broadcast_to(a.astype(F32)[..., None, :], a.shape[:-1] + (SUBLANES, a.shape[-1]))
    n_post, m_post = row(mix_norm_post), row(mlp_norm_post)
    ln_g, ln_b, p_scale = rep(conf_ln_g), rep(conf_ln_b), row(pool_scale)
    conf_dw, sc_dw = rep(conf_dw), rep(sc_dw)

    xt = x.reshape(b * s, d)
    for l in range(depth):
        if l == 0:
            loc, h, rs = _in_proj((xt,), w_loc, l)
        else:
            (loc,) = _in_proj((h, rs), w_loc, l)
        qkv, gate = _in_proj((h, rs), w_qkv, l, w_gate)
        fcol, frow = _gate_prefix(gate, b_gate[l], b, s)
        y_att = _attention(qkv, fcol, frow, b, s)
        y_loc = _local_mixers(loc, conf_dw[l], ln_g[l], ln_b[l], pool_w_b[l], p_scale[l], sc_dw[l], b, s)
        xt, h, rs = _out_proj(y_att, y_loc, w_out_b, l, xt, n_post[l])
        xt, h, rs = _mlp(xt, h, rs, w1_b, w2_b, l, m_post[l])
    return xt.reshape(b, s, d)
```

```python
import jax
import jax.numpy as jnp
from jax import lax
from jax.experimental import pallas as pl
from jax.experimental.pallas import tpu as pltpu

F32 = jnp.float32
BF16 = jnp.bfloat16

EPS = 1e-6
LOG2E = 1.4426950408889634
N_HEADS = 4
HEAD_DIM = 128
GROUP_W = N_HEADS * HEAD_DIM
CONF_K = 31
SC_K = 3
POOL_WINDOWS = (2, 4, 8, 16)
POOL_GROUP = GROUP_W // len(POOL_WINDOWS)
LANES = 128
SUBLANES = 8
HALO = 32
N_LOCAL_COLS = 6 * GROUP_W
VMEM_LIMIT = 56 * 1024 * 1024

TM_IN, TN_IN = 1024, 768
TM_IN_F32 = 512
TM_OUT = 512
TM_MLP, TF_MLP = 512, 1024
SUB = 128
TS_LOCAL = 512
ROWS = 32
TQ = 256


def _params(*sem):
    return pltpu.CompilerParams(dimension_semantics=sem, vmem_limit_bytes=VMEM_LIMIT)


def _lanes(v, width):
    return jnp.concatenate([v] * (width // LANES), axis=1)


def _inv_rms(v):
    ms = jnp.mean(v * v, axis=-1, keepdims=True)
    return jnp.broadcast_to(lax.rsqrt(ms + EPS), (v.shape[0], LANES))


def _emit_stream(x_new, r0, x_ref, h_ref, rs_ref):
    rows = x_new.shape[0]
    x_ref[r0:r0 + rows, :] = x_new
    h_ref[r0:r0 + rows, :] = x_new.astype(BF16)
    rs_ref[r0:r0 + rows, :] = _inv_rms(x_new)


def _in_proj_kernel(h_ref, rs_ref, wl_ref, wq_ref, wg_ref, loc_ref, qkv_ref, gate_ref):
    rs = rs_ref[...]
    for w_ref, o_ref in ((wl_ref, loc_ref), (wq_ref, qkv_ref)):
        for c0 in range(0, o_ref.shape[1], TN_IN):
            acc = jnp.dot(h_ref[...], w_ref[:, c0:c0 + TN_IN], preferred_element_type=F32)
            o_ref[:, c0:c0 + TN_IN] = (acc * _lanes(rs, TN_IN)).astype(BF16)
    gate_ref[...] = jnp.dot(h_ref[...], wg_ref[...], preferred_element_type=F32) * rs


def _in_proj_f32_kernel(x_ref, wl_ref, wq_ref, wg_ref, loc_ref, qkv_ref, gate_ref, h_ref, rs_ref):
    for r0 in range(0, x_ref.shape[0], SUB):
        x = x_ref[r0:r0 + SUB, :]
        h_ref[r0:r0 + SUB, :] = x.astype(BF16)
        rs_ref[r0:r0 + SUB, :] = _inv_rms(x)
    _in_proj_kernel(h_ref, rs_ref, wl_ref, wq_ref, wg_ref, loc_ref, qkv_ref, gate_ref)


def _in_proj(stream, w_loc, w_all, layer, n_qkv, gate_block):
    t, d = stream[0].shape
    first = len(stream) == 1
    tm = min(TM_IN_F32 if first else TM_IN, t)
    n_loc = w_loc.shape[2]
    row_tile = lambda width: pl.BlockSpec((tm, width), lambda i: (i, 0))
    resident = lambda width, blk: pl.BlockSpec((None, d, width), lambda i: (layer, 0, blk),
                                               pipeline_mode=pl.Buffered(1))
    in_specs = [row_tile(d)] + ([] if first else [row_tile(LANES)])
    in_specs += [resident(n_loc, 0), resident(n_qkv, 0), resident(LANES, gate_block)]
    out_specs = [row_tile(n_loc), row_tile(n_qkv), row_tile(LANES)]
    out_shape = [jax.ShapeDtypeStruct((t, n_loc), BF16), jax.ShapeDtypeStruct((t, n_qkv), BF16),
                 jax.ShapeDtypeStruct((t, LANES), F32)]
    if first:
        out_specs += [row_tile(d), row_tile(LANES)]
        out_shape += [jax.ShapeDtypeStruct((t, d), BF16), jax.ShapeDtypeStruct((t, LANES), F32)]
    return pl.pallas_call(
        _in_proj_f32_kernel if first else _in_proj_kernel,
        grid=(t // tm,),
        in_specs=in_specs,
        out_specs=out_specs,
        out_shape=out_shape,
        compiler_params=_params("parallel"),
        name="in_proj",
    )(*stream, w_loc, w_all, w_all)


def _split3(x):
    hi = x.astype(BF16)
    r1 = x - hi.astype(F32)
    mid = r1.astype(BF16)
    lo = (r1 - mid.astype(F32)).astype(BF16)
    return hi, mid, lo


def _gate_kernel(g_ref, b_ref, fcol_ref, frow_ref):
    s = g_ref.shape[0]
    row = lax.broadcasted_iota(jnp.int32, (LANES, LANES), 0)
    col = lax.broadcasted_iota(jnp.int32, (LANES, LANES), 1)
    tri = jnp.where(row >= col, 1.0, 0.0).astype(BF16)
    carry = jnp.zeros((1, LANES), F32)
    for j in range(s // LANES):
        z = g_ref[j * LANES:(j + 1) * LANES, :] + b_ref[...]
        logf = (jnp.minimum(z, 0.0) - jnp.log1p(jnp.exp(-jnp.abs(z)))) * LOG2E
        hi, mid, lo = _split3(logf)
        cum = (jnp.dot(tri, hi, preferred_element_type=F32)
               + jnp.dot(tri, mid, preferred_element_type=F32)
               + jnp.dot(tri, lo, preferred_element_type=F32)) + carry
        carry = cum[LANES - 1:LANES, :]
        fcol_ref[j * LANES:(j + 1) * LANES, :] = cum
        frow_ref[:, j * LANES:(j + 1) * LANES] = cum.T[0:8, :]


def _gate_prefix(gate, bias, b, s):
    return pl.pallas_call(
        _gate_kernel,
        grid=(b,),
        in_specs=[
            pl.BlockSpec((s, LANES), lambda i: (i, 0)),
            pl.BlockSpec((1, LANES), lambda i: (0, 0)),
        ],
        out_specs=[
            pl.BlockSpec((s, LANES), lambda i: (i, 0)),
            pl.BlockSpec((None, 8, s), lambda i: (i, 0, 0)),
        ],
        out_shape=[
            jax.ShapeDtypeStruct((b * s, LANES), F32),
            jax.ShapeDtypeStruct((b, 8, s), F32),
        ],
        compiler_params=_params("parallel"),
        name="gate_prefix",
    )(gate, bias)


def _attn_kernel(q_ref, k_ref, v_ref, fcol_ref, frow_ref, o_ref, s_ref, p_ref):
    s_len = q_ref.shape[0]
    h = pl.program_id(1)
    lane = lax.broadcasted_iota(jnp.int32, (s_len, LANES), 1)
    fq_all = jnp.sum(jnp.where(lane == h, fcol_ref[...], 0.0), axis=1, keepdims=True)
    fk_all = frow_ref[pl.ds(h, 1), :]
    nt = (((1,), (1,)), ((), ()))
    n_q = s_len // TQ

    def score_chunk(i, c, m_part):
        q0, c0 = i * TQ, c * TQ
        sc = lax.dot_general(q_ref[q0:q0 + TQ, :], k_ref[c0:c0 + TQ, :], nt, preferred_element_type=F32)
        sc = sc + (fq_all[q0:q0 + TQ, :] - fk_all[:, c0:c0 + TQ])
        if c == i:
            r_id = lax.broadcasted_iota(jnp.int32, (TQ, TQ), 0)
            c_id = lax.broadcasted_iota(jnp.int32, (TQ, TQ), 1)
            sc = jnp.where(c_id <= r_id, sc, -jnp.inf)
        s_ref[i % 2, :, c0:c0 + TQ] = sc
        for t in range(TQ // LANES):
            m_part = jnp.maximum(m_part, sc[:, t * LANES:(t + 1) * LANES])
        return m_part

    def weight_chunk(i, c, m, l_part):
        c0 = c * TQ
        p = jnp.exp2(s_ref[i % 2, :, c0:c0 + TQ] - m)
        for t in range(TQ // LANES):
            l_part = l_part + p[:, t * LANES:(t + 1) * LANES]
        p_ref[i % 2, :, c0:c0 + TQ] = p.astype(BF16)
        return l_part

    neg_inf = jnp.full((TQ, LANES), -jnp.inf, F32)
    m_cur = jnp.max(score_chunk(0, 0, neg_inf), axis=1, keepdims=True)
    for i in range(n_q):
        l_part = jnp.zeros((TQ, LANES), F32)
        m_part = neg_inf
        for c in range(i + 2):
            if i + 1 < n_q:
                m_part = score_chunk(i + 1, c, m_part)
            if c <= i:
                l_part = weight_chunk(i, c, m_cur, l_part)
        l = jnp.sum(l_part, axis=1, keepdims=True)
        kv = (i + 1) * TQ
        o = jnp.dot(p_ref[i % 2, :, 0:kv], v_ref[0:kv, :], preferred_element_type=F32)
        o_ref[i * TQ:kv, :] = (o / l).astype(BF16)
        m_cur = jnp.max(m_part, axis=1, keepdims=True)


def _attention(qkv, fcol, frow, b, s):
    t = b * s
    return pl.pallas_call(
        _attn_kernel,
        grid=(b, N_HEADS),
        in_specs=[
            pl.BlockSpec((s, HEAD_DIM), lambda i, h: (i, h)),
            pl.BlockSpec((s, HEAD_DIM), lambda i, h: (i, N_HEADS + h)),
            pl.BlockSpec((s, HEAD_DIM), lambda i, h: (i, 2 * N_HEADS + h)),
            pl.BlockSpec((s, LANES), lambda i, h: (i, 0)),
            pl.BlockSpec((None, 8, s), lambda i, h: (i, 0, 0)),
        ],
        out_specs=pl.BlockSpec((s, HEAD_DIM), lambda i, h: (i, h)),
        out_shape=jax.ShapeDtypeStruct((t, GROUP_W), BF16),
        scratch_shapes=[pltpu.VMEM((2, TQ, s), F32), pltpu.VMEM((2, TQ, s), BF16)],
        compiler_params=_params("parallel", "arbitrary"),
        name="fox_attention",
    )(qkv, qkv, qkv, fcol, frow)


def _local_body(i, main_ref, halo_ref, cw_ref, lng_ref, lnb_ref, pw_ref, ps_ref, sw_ref,
                o_ref, u_ref, p_ref, z_ref, pooled_ref):
    ts = main_ref.shape[0]
    g = GROUP_W

    def rows(v):
        return jnp.concatenate([v] * (ROWS // SUBLANES), axis=0)

    def fill(dst0, src):
        src = src.astype(F32)
        a, gt = src[:, 0:g], src[:, g:2 * g]
        u_ref[0, dst0:dst0 + src.shape[0], :] = a * jax.nn.sigmoid(gt)
        p_ref[dst0:dst0 + src.shape[0], :] = src[:, 2 * g:3 * g]
        z_ref[dst0:dst0 + src.shape[0], :] = src[:, 4 * g:5 * g] * src[:, 5 * g:6 * g]

    halo = jnp.where(i > 0, halo_ref[...], jnp.zeros_like(halo_ref))
    fill(0, halo)
    for r in range(0, ts, ROWS):
        fill(HALO + r, main_ref[r:r + ROWS, :])
    n_ext = HALO + ts
    for sft in range(1, SUBLANES):
        for r in range(SUBLANES, n_ext, 4 * ROWS):
            r1 = min(r + 4 * ROWS, n_ext)
            u_ref[sft, r:r1, :] = u_ref[0, r - sft:r1 - sft, :]

    row_id = lax.broadcasted_iota(jnp.int32, (ROWS, POOL_GROUP), 0)
    for r in range(0, ts, ROWS):
        e0 = HALO + r
        acc = jnp.zeros((ROWS, g), F32)
        for k in range(CONF_K):
            back8, sft = divmod(CONF_K - 1 - k, SUBLANES)
            off = e0 - SUBLANES * back8
            acc = acc + rows(cw_ref[k]) * u_ref[sft, off:off + ROWS, :]
        mu = jnp.mean(acc, axis=-1, keepdims=True)
        cen = acc - mu
        var = jnp.mean(cen * cen, axis=-1, keepdims=True)
        y = cen * lax.rsqrt(var + EPS) * rows(lng_ref[...]) + rows(lnb_ref[...])
        o_ref[r:r + ROWS, 0:g] = (y * jax.nn.sigmoid(y)).astype(BF16)

        pos = i * ts + r + row_id
        for gi, w in enumerate(POOL_WINDOWS):
            c0 = gi * POOL_GROUP
            xt = p_ref[e0:e0 + ROWS, c0:c0 + POOL_GROUP]
            win = xt
            for jj in range(1, w):
                win = win + p_ref[e0 - jj:e0 - jj + ROWS, c0:c0 + POOL_GROUP]
            cnt = jnp.minimum(pos + 1, w).astype(F32)
            pooled_ref[r:r + ROWS, c0:c0 + POOL_GROUP] = (win / cnt - xt).astype(BF16)

        conv = jnp.zeros((ROWS, g), F32)
        for k in range(SC_K):
            off = e0 - (SC_K - 1) + k
            conv = conv + rows(sw_ref[k]) * z_ref[off:off + ROWS, :]
        bg = main_ref[r:r + ROWS, 3 * g:4 * g].astype(F32)
        o_ref[r:r + ROWS, 2 * g:3 * g] = (bg * conv).astype(BF16)

    for gi in range(len(POOL_WINDOWS)):
        c0 = gi * POOL_GROUP
        y = jnp.dot(pooled_ref[:, c0:c0 + POOL_GROUP], pw_ref[gi], preferred_element_type=F32)
        o_ref[:, g + c0:g + c0 + POOL_GROUP] = (y * ps_ref[:, c0:c0 + POOL_GROUP]).astype(BF16)


def _local_kernel(*refs):
    _local_body(pl.program_id(1), *refs)


def _local_mixers(proj, conf_dw, ln_g, ln_b, pool_w, pool_scale, sc_dw, b, s):
    t = b * s
    ts = min(TS_LOCAL, s)
    nst = s // ts
    hb = ts // HALO
    g = GROUP_W

    def halo_idx(bi, si):
        return (jnp.maximum(bi * nst * hb + si * hb - 1, 0), 0)

    full = lambda shape: pl.BlockSpec(shape, lambda bi, si: (0,) * len(shape))
    return pl.pallas_call(
        _local_kernel,
        grid=(b, nst),
        in_specs=[
            pl.BlockSpec((ts, N_LOCAL_COLS), lambda bi, si: (bi * nst + si, 0)),
            pl.BlockSpec((HALO, N_LOCAL_COLS), halo_idx),
            full((CONF_K, SUBLANES, g)), full((SUBLANES, g)), full((SUBLANES, g)),
            full((len(POOL_WINDOWS), POOL_GROUP, POOL_GROUP)), full((1, g)), full((SC_K, SUBLANES, g)),
        ],
        out_specs=pl.BlockSpec((ts, 3 * g), lambda bi, si: (bi * nst + si, 0)),
        out_shape=jax.ShapeDtypeStruct((t, 3 * g), BF16),
        scratch_shapes=[pltpu.VMEM((SUBLANES, HALO + ts, g), F32)] + [pltpu.VMEM((HALO + ts, g), F32)] * 2
        + [pltpu.VMEM((ts, g), BF16)],
        compiler_params=_params("parallel", "parallel"),
        name="local_mixers",
    )(proj, proj, conf_dw, ln_g, ln_b, pool_w, pool_scale, sc_dw)


def _post_norm_residual(y, x, gain):
    d = y.shape[1]
    return x + y * _lanes(_inv_rms(y), d) * gain


def _stream_out(t, d, tm, idx):
    specs = [pl.BlockSpec((tm, d), idx), pl.BlockSpec((tm, d), idx), pl.BlockSpec((tm, LANES), idx)]
    shapes = [jax.ShapeDtypeStruct((t, d), F32), jax.ShapeDtypeStruct((t, d), BF16),
              jax.ShapeDtypeStruct((t, LANES), F32)]
    return specs, shapes


def _out_proj_kernel(ya_ref, yl_ref, w_ref, x_ref, g_ref, xo_ref, ho_ref, rso_ref):
    g = GROUP_W

    def project(r0, n):
        return (jnp.dot(ya_ref[r0:r0 + n, :], w_ref[0:g, :], preferred_element_type=F32)
                + jnp.dot(yl_ref[r0:r0 + n, :], w_ref[g:, :], preferred_element_type=F32))

    sizes = [SUB] * (x_ref.shape[0] // SUB)
    starts = [sum(sizes[:k]) for k in range(len(sizes))]
    y_next = project(starts[0], sizes[0])
    for k, (r0, n) in enumerate(zip(starts, sizes)):
        y = y_next
        if k + 1 < len(starts):
            y_next = project(starts[k + 1], sizes[k + 1])
        x_new = _post_norm_residual(y, x_ref[r0:r0 + n, :], g_ref[...])
        _emit_stream(x_new, r0, xo_ref, ho_ref, rso_ref)


def _out_proj(y_att, y_loc, w_out, layer, x, gain):
    t, d = x.shape
    tm = min(TM_OUT, t)
    out_specs, out_shape = _stream_out(t, d, tm, lambda i: (i, 0))
    return pl.pallas_call(
        _out_proj_kernel,
        grid=(t // tm,),
        in_specs=[
            pl.BlockSpec((tm, GROUP_W), lambda i: (i, 0)),
            pl.BlockSpec((tm, 3 * GROUP_W), lambda i: (i, 0)),
            pl.BlockSpec((None, d, d), lambda i: (layer, 0, 0)),
            pl.BlockSpec((tm, d), lambda i: (i, 0)),
            pl.BlockSpec((1, d), lambda i: (0, 0)),
        ],
        out_specs=out_specs,
        out_shape=out_shape,
        compiler_params=_params("parallel"),
        name="out_proj",
    )(y_att, y_loc, w_out, x, gain)


def _mlp_kernel(x_ref, h_ref, rs_ref, w1_ref, w2_ref, g_ref, xo_ref, ho_ref, rso_ref):
    f = pl.program_id(1)
    last = pl.num_programs(1) - 1

    def hidden():
        u = jnp.dot(h_ref[...], w1_ref[...], preferred_element_type=F32)
        return jnp.square(jnp.maximum(u, 0.0)).astype(BF16)

    @pl.when(f == 0)
    def _():
        xo_ref[...] = jnp.dot(hidden(), w2_ref[...], preferred_element_type=F32)

    @pl.when((f > 0) & (f < last))
    def _():
        xo_ref[...] += jnp.dot(hidden(), w2_ref[...], preferred_element_type=F32)

    @pl.when(f == last)
    def _():
        d = xo_ref.shape[1]
        u = hidden()

        def total(r0):
            return xo_ref[r0:r0 + SUB, :] + jnp.dot(u[r0:r0 + SUB, :], w2_ref[...], preferred_element_type=F32)

        starts = list(range(0, x_ref.shape[0], SUB))
        acc_next = total(starts[0])
        for k, r0 in enumerate(starts):
            acc = acc_next
            if k + 1 < len(starts):
                acc_next = total(starts[k + 1])
            r2 = rs_ref[r0:r0 + SUB, :] * rs_ref[r0:r0 + SUB, :]
            ms = jnp.broadcast_to(jnp.mean(acc * acc, axis=-1, keepdims=True), (SUB, LANES))
            scale = r2 * lax.rsqrt(r2 * r2 * ms + EPS)
            x_new = x_ref[r0:r0 + SUB, :] + acc * _lanes(scale, d) * g_ref[...]
            _emit_stream(x_new, r0, xo_ref, ho_ref, rso_ref)


def _mlp(x, h, rs, w1, w2, layer, gain):
    t, d = x.shape
    tf = TF_MLP
    assert w1.shape[2] // tf >= 2, "the MLP kernel needs separate first and last d_ff chunks"
    tm = min(TM_MLP, t)
    out_specs, out_shape = _stream_out(t, d, tm, lambda i, f: (i, 0))
    return pl.pallas_call(
        _mlp_kernel,
        grid=(t // tm, w1.shape[2] // tf),
        in_specs=[
            pl.BlockSpec((tm, d), lambda i, f: (i, 0)),
            pl.BlockSpec((tm, d), lambda i, f: (i, 0)),
            pl.BlockSpec((tm, LANES), lambda i, f: (i, 0)),
            pl.BlockSpec((None, d, tf), lambda i, f: (layer, 0, f)),
            pl.BlockSpec((None, tf, d), lambda i, f: (layer, f, 0)),
            pl.BlockSpec((1, d), lambda i, f: (0, 0)),
        ],
        out_specs=out_specs,
        out_shape=out_shape,
        compiler_params=_params("parallel", "arbitrary"),
        name="mlp",
    )(x, h, rs, w1, w2, gain)


def kernel(x, mix_norm_pre, w_in, b_forget, conf_dw, conf_ln_g, conf_ln_b, pool_w, pool_scale,
           sc_dw, w_out, mix_norm_post, mlp_norm_pre, w_mlp1, w_mlp2, mlp_norm_post):
    b, s, d = x.shape
    depth = w_in.shape[0]
    g = GROUP_W
    o_gate = 3 * g
    o_conf = o_gate + N_HEADS
    col_scale = jnp.where(jnp.arange(w_in.shape[-1]) < g, LOG2E / (HEAD_DIM ** 0.5), 1.0).astype(F32)
    w_in_b = (w_in * mix_norm_pre.astype(F32)[:, :, None] * col_scale).astype(BF16)
    w_loc = w_in_b[:, :, o_conf:]
    b_gate = jnp.pad(b_forget.astype(F32), ((0, 0), (0, LANES - N_HEADS)))[:, None, :]
    w_out_b = w_out.astype(BF16)
    w1_b = (w_mlp1 * mlp_norm_pre.astype(F32)[:, :, None]).astype(BF16)
    w2_b = w_mlp2.astype(BF16)
    pool_w_b = pool_w.astype(BF16)
    row = lambda a: a.astype(F32)[:, None, :]
    rep = lambda a: jnp.broadcast_to(a.astype(F32)[..., None, :], a.shape[:-1] + (SUBLANES, a.shape[-1]))
    n_post, m_post = row(mix_norm_post), row(mlp_norm_post)
    ln_g, ln_b, p_scale = rep(conf_ln_g), rep(conf_ln_b), row(pool_scale)
    conf_dw, sc_dw = rep(conf_dw), rep(sc_dw)

    xt = x.reshape(b * s, d)
    for l in range(depth):
        stream = (xt,) if l == 0 else (h, rs)
        loc, qkv, gate, *first = _in_proj(stream, w_loc, w_in_b, l, o_gate, o_gate // LANES)
        if l == 0:
            h, rs = first
        fcol, frow = _gate_prefix(gate, b_gate[l], b, s)
        y_att = _attention(qkv, fcol, frow, b, s)
        y_loc = _local_mixers(loc, conf_dw[l], ln_g[l], ln_b[l], pool_w_b[l], p_scale[l], sc_dw[l], b, s)
        xt, h, rs = _out_proj(y_att, y_loc, w_out_b, l, xt, n_post[l])
        xt, h, rs = _mlp(xt, h, rs, w1_b, w2_b, l, m_post[l])
    return xt.reshape(b, s, d)
```

```python
import jax
import jax.numpy as jnp
from jax import lax
from jax.experimental import pallas as pl
from jax.experimental.pallas import tpu as pltpu

F32 = jnp.float32
BF16 = jnp.bfloat16

EPS = 1e-6
LOG2E = 1.4426950408889634
N_HEADS = 4
HEAD_DIM = 128
GROUP_W = N_HEADS * HEAD_DIM
CONF_K = 31
SC_K = 3
POOL_WINDOWS = (2, 4, 8, 16)
POOL_GROUP = GROUP_W // len(POOL_WINDOWS)
LANES = 128
SUBLANES = 8
HALO = 32
N_LOCAL_COLS = 6 * GROUP_W
VMEM_LIMIT = 56 * 1024 * 1024

TM_IN, TN_IN = 1024, 768
TM_IN_F32 = 512
TM_OUT = 512
TM_MLP, TF_MLP = 512, 1024
SUB = 128
TS_LOCAL = 512
ROWS = 32
TQ = 256


def _params(*sem):
    return pltpu.CompilerParams(dimension_semantics=sem, vmem_limit_bytes=VMEM_LIMIT)


def _lanes(v, width):
    return jnp.concatenate([v] * (width // LANES), axis=1)


def _inv_rms(v):
    ms = jnp.mean(v * v, axis=-1, keepdims=True)
    return jnp.broadcast_to(lax.rsqrt(ms + EPS), (v.shape[0], LANES))


def _emit_stream(x_new, r0, x_ref, h_ref, rs_ref):
    rows = x_new.shape[0]
    x_ref[r0:r0 + rows, :] = x_new
    h_ref[r0:r0 + rows, :] = x_new.astype(BF16)
    rs_ref[r0:r0 + rows, :] = _inv_rms(x_new)


def _in_proj_kernel(h_ref, rs_ref, wl_ref, wq_ref, wg_ref, loc_ref, qkv_ref, gate_ref):
    rs = rs_ref[...]
    for w_ref, o_ref in ((wl_ref, loc_ref), (wq_ref, qkv_ref)):
        for c0 in range(0, o_ref.shape[1], TN_IN):
            acc = jnp.dot(h_ref[...], w_ref[:, c0:c0 + TN_IN], preferred_element_type=F32)
            o_ref[:, c0:c0 + TN_IN] = (acc * _lanes(rs, TN_IN)).astype(BF16)
    gate_ref[...] = jnp.dot(h_ref[...], wg_ref[...], preferred_element_type=F32) * rs


def _in_proj_f32_kernel(x_ref, wl_ref, wq_ref, wg_ref, loc_ref, qkv_ref, gate_ref, h_ref, rs_ref):
    for r0 in range(0, x_ref.shape[0], SUB):
        x = x_ref[r0:r0 + SUB, :]
        h_ref[r0:r0 + SUB, :] = x.astype(BF16)
        rs_ref[r0:r0 + SUB, :] = _inv_rms(x)
    _in_proj_kernel(h_ref, rs_ref, wl_ref, wq_ref, wg_ref, loc_ref, qkv_ref, gate_ref)


def _in_proj(stream, w_loc, w_all, layer, n_qkv, gate_block):
    t, d = stream[0].shape
    first = len(stream) == 1
    tm = min(TM_IN_F32 if first else TM_IN, t)
    n_loc = w_loc.shape[2]
    row_tile = lambda width: pl.BlockSpec((tm, width), lambda i: (i, 0))
    resident = lambda width, blk: pl.BlockSpec((None, d, width), lambda i: (layer, 0, blk),
                                               pipeline_mode=pl.Buffered(1))
    in_specs = [row_tile(d)] + ([] if first else [row_tile(LANES)])
    in_specs += [resident(n_loc, 0), resident(n_qkv, 0), resident(LANES, gate_block)]
    out_specs = [row_tile(n_loc), row_tile(n_qkv), row_tile(LANES)]
    out_shape = [jax.ShapeDtypeStruct((t, n_loc), BF16), jax.ShapeDtypeStruct((t, n_qkv), BF16),
                 jax.ShapeDtypeStruct((t, LANES), F32)]
    if first:
        out_specs += [row_tile(d), row_tile(LANES)]
        out_shape += [jax.ShapeDtypeStruct((t, d), BF16), jax.ShapeDtypeStruct((t, LANES), F32)]
    return pl.pallas_call(
        _in_proj_f32_kernel if first else _in_proj_kernel,
        grid=(t // tm,),
        in_specs=in_specs,
        out_specs=out_specs,
        out_shape=out_shape,
        compiler_params=_params("parallel"),
        name="in_proj",
    )(*stream, w_loc, w_all, w_all)


def _split3(x):
    hi = x.astype(BF16)
    r1 = x - hi.astype(F32)
    mid = r1.astype(BF16)
    lo = (r1 - mid.astype(F32)).astype(BF16)
    return hi, mid, lo


def _gate_kernel(g_ref, b_ref, fcol_ref, frow_ref):
    s = g_ref.shape[0]
    row = lax.broadcasted_iota(jnp.int32, (LANES, LANES), 0)
    col = lax.broadcasted_iota(jnp.int32, (LANES, LANES), 1)
    tri = jnp.where(row >= col, 1.0, 0.0).astype(BF16)
    carry = jnp.zeros((1, LANES), F32)
    for j in range(s // LANES):
        z = g_ref[j * LANES:(j + 1) * LANES, :] + b_ref[...]
        logf = (jnp.minimum(z, 0.0) - jnp.log1p(jnp.exp(-jnp.abs(z)))) * LOG2E
        hi, mid, lo = _split3(logf)
        cum = (jnp.dot(tri, hi, preferred_element_type=F32)
               + jnp.dot(tri, mid, preferred_element_type=F32)
               + jnp.dot(tri, lo, preferred_element_type=F32)) + carry
        carry = cum[LANES - 1:LANES, :]
        fcol_ref[j * LANES:(j + 1) * LANES, :] = cum
        frow_ref[:, j * LANES:(j + 1) * LANES] = cum.T[0:8, :]


def _gate_prefix(gate, bias, b, s):
    return pl.pallas_call(
        _gate_kernel,
        grid=(b,),
        in_specs=[
            pl.BlockSpec((s, LANES), lambda i: (i, 0)),
            pl.BlockSpec((1, LANES), lambda i: (0, 0)),
        ],
        out_specs=[
            pl.BlockSpec((s, LANES), lambda i: (i, 0)),
            pl.BlockSpec((None, 8, s), lambda i: (i, 0, 0)),
        ],
        out_shape=[
            jax.ShapeDtypeStruct((b * s, LANES), F32),
            jax.ShapeDtypeStruct((b, 8, s), F32),
        ],
        compiler_params=_params("parallel"),
        name="gate_prefix",
    )(gate, bias)


def _attn_kernel(q_ref, k_ref, v_ref, fcol_ref, frow_ref, o_ref, s_ref, p_ref):
    s_len = q_ref.shape[0]
    h = pl.program_id(1)
    lane = lax.broadcasted_iota(jnp.int32, (s_len, LANES), 1)
    fq_all = jnp.sum(jnp.where(lane == h, fcol_ref[...], 0.0), axis=1, keepdims=True)
    fk_all = frow_ref[pl.ds(h, 1), :]
    nt = (((1,), (1,)), ((), ()))
    n_q = s_len // TQ

    def score_chunk(i, c, m_part):
        q0, c0 = i * TQ, c * TQ
        sc = lax.dot_general(q_ref[q0:q0 + TQ, :], k_ref[c0:c0 + TQ, :], nt, preferred_element_type=F32)
        sc = sc + (fq_all[q0:q0 + TQ, :] - fk_all[:, c0:c0 + TQ])
        if c == i:
            r_id = lax.broadcasted_iota(jnp.int32, (TQ, TQ), 0)
            c_id = lax.broadcasted_iota(jnp.int32, (TQ, TQ), 1)
            sc = jnp.where(c_id <= r_id, sc, -jnp.inf)
        s_ref[i % 2, :, c0:c0 + TQ] = sc
        for t in range(TQ // LANES):
            m_part = jnp.maximum(m_part, sc[:, t * LANES:(t + 1) * LANES])
        return m_part

    def weight_chunk(i, c, m, l_part):
        c0 = c * TQ
        p = jnp.exp2(s_ref[i % 2, :, c0:c0 + TQ] - m)
        for t in range(TQ // LANES):
            l_part = l_part + p[:, t * LANES:(t + 1) * LANES]
        p_ref[i % 2, :, c0:c0 + TQ] = p.astype(BF16)
        return l_part

    neg_inf = jnp.full((TQ, LANES), -jnp.inf, F32)
    m_cur = jnp.max(score_chunk(0, 0, neg_inf), axis=1, keepdims=True)
    for i in range(n_q):
        l_part = jnp.zeros((TQ, LANES), F32)
        m_part = neg_inf
        for c in range(i + 2):
            if i + 1 < n_q:
                m_part = score_chunk(i + 1, c, m_part)
            if c <= i:
                l_part = weight_chunk(i, c, m_cur, l_part)
        l = jnp.sum(l_part, axis=1, keepdims=True)
        kv = (i + 1) * TQ
        o = jnp.dot(p_ref[i % 2, :, 0:kv], v_ref[0:kv, :], preferred_element_type=F32)
        o_ref[i * TQ:kv, :] = (o / l).astype(BF16)
        m_cur = jnp.max(m_part, axis=1, keepdims=True)


def _attention(qkv, fcol, frow, b, s):
    t = b * s
    return pl.pallas_call(
        _attn_kernel,
        grid=(b, N_HEADS),
        in_specs=[
            pl.BlockSpec((s, HEAD_DIM), lambda i, h: (i, h)),
            pl.BlockSpec((s, HEAD_DIM), lambda i, h: (i, N_HEADS + h)),
            pl.BlockSpec((s, HEAD_DIM), lambda i, h: (i, 2 * N_HEADS + h)),
            pl.BlockSpec((s, LANES), lambda i, h: (i, 0)),
            pl.BlockSpec((None, 8, s), lambda i, h: (i, 0, 0)),
        ],
        out_specs=pl.BlockSpec((s, HEAD_DIM), lambda i, h: (i, h)),
        out_shape=jax.ShapeDtypeStruct((t, GROUP_W), BF16),
        scratch_shapes=[pltpu.VMEM((2, TQ, s), F32), pltpu.VMEM((2, TQ, s), BF16)],
        compiler_params=_params("parallel", "arbitrary"),
        name="fox_attention",
    )(qkv, qkv, qkv, fcol, frow)


def _local_body(i, main_ref, halo_ref, cw_ref, lng_ref, lnb_ref, pw_ref, ps_ref, sw_ref,
                o_ref, u_ref, p_ref, z_ref, pooled_ref):
    ts = main_ref.shape[0]
    g = GROUP_W

    def rows(v):
        return jnp.concatenate([v] * (ROWS // SUBLANES), axis=0)

    def fill(dst0, src):
        src = src.astype(F32)
        a, gt = src[:, 0:g], src[:, g:2 * g]
        u_ref[0, dst0:dst0 + src.shape[0], :] = a * jax.nn.sigmoid(gt)
        p_ref[dst0:dst0 + src.shape[0], :] = src[:, 2 * g:3 * g]
        z_ref[dst0:dst0 + src.shape[0], :] = src[:, 4 * g:5 * g] * src[:, 5 * g:6 * g]

    halo = jnp.where(i > 0, halo_ref[...], jnp.zeros_like(halo_ref))
    fill(0, halo)
    for r in range(0, ts, ROWS):
        fill(HALO + r, main_ref[r:r + ROWS, :])
    n_ext = HALO + ts
    for sft in range(1, SUBLANES):
        for r in range(SUBLANES, n_ext, 4 * ROWS):
            r1 = min(r + 4 * ROWS, n_ext)
            u_ref[sft, r:r1, :] = u_ref[0, r - sft:r1 - sft, :]

    row_id = lax.broadcasted_iota(jnp.int32, (ROWS, POOL_GROUP), 0)
    for r in range(0, ts, ROWS):
        e0 = HALO + r
        acc = jnp.zeros((ROWS, g), F32)
        for k in range(CONF_K):
            back8, sft = divmod(CONF_K - 1 - k, SUBLANES)
            off = e0 - SUBLANES * back8
            acc = acc + rows(cw_ref[k]) * u_ref[sft, off:off + ROWS, :]
        mu = jnp.mean(acc, axis=-1, keepdims=True)
        cen = acc - mu
        var = jnp.mean(cen * cen, axis=-1, keepdims=True)
        y = cen * lax.rsqrt(var + EPS) * rows(lng_ref[...]) + rows(lnb_ref[...])
        o_ref[r:r + ROWS, 0:g] = (y * jax.nn.sigmoid(y)).astype(BF16)

        pos = i * ts + r + row_id
        for gi, w in enumerate(POOL_WINDOWS):
            c0 = gi * POOL_GROUP
            xt = p_ref[e0:e0 + ROWS, c0:c0 + POOL_GROUP]
            win = xt
            for jj in range(1, w):
                win = win + p_ref[e0 - jj:e0 - jj + ROWS, c0:c0 + POOL_GROUP]
            cnt = jnp.minimum(pos + 1, w).astype(F32)
            pooled_ref[r:r + ROWS, c0:c0 + POOL_GROUP] = (win / cnt - xt).astype(BF16)

        conv = jnp.zeros((ROWS, g), F32)
        for k in range(SC_K):
            off = e0 - (SC_K - 1) + k
            conv = conv + rows(sw_ref[k]) * z_ref[off:off + ROWS, :]
        bg = main_ref[r:r + ROWS, 3 * g:4 * g].astype(F32)
        o_ref[r:r + ROWS, 2 * g:3 * g] = (bg * conv).astype(BF16)

    for gi in range(len(POOL_WINDOWS)):
        c0 = gi * POOL_GROUP
        y = jnp.dot(pooled_ref[:, c0:c0 + POOL_GROUP], pw_ref[gi], preferred_element_type=F32)
        o_ref[:, g + c0:g + c0 + POOL_GROUP] = (y * ps_ref[:, c0:c0 + POOL_GROUP]).astype(BF16)


def _local_kernel(*refs):
    _local_body(pl.program_id(1), *refs)


def _local_mixers(proj, conf_dw, ln_g, ln_b, pool_w, pool_scale, sc_dw, b, s):
    t = b * s
    ts = min(TS_LOCAL, s)
    nst = s // ts
    hb = ts // HALO
    g = GROUP_W

    def halo_idx(bi, si):
        return (jnp.maximum(bi * nst * hb + si * hb - 1, 0), 0)

    full = lambda shape: pl.BlockSpec(shape, lambda bi, si: (0,) * len(shape))
    return pl.pallas_call(
        _local_kernel,
        grid=(b, nst),
        in_specs=[
            pl.BlockSpec((ts, N_LOCAL_COLS), lambda bi, si: (bi * nst + si, 0)),
            pl.BlockSpec((HALO, N_LOCAL_COLS), halo_idx),
            full((CONF_K, SUBLANES, g)), full((SUBLANES, g)), full((SUBLANES, g)),
            full((len(POOL_WINDOWS), POOL_GROUP, POOL_GROUP)), full((1, g)), full((SC_K, SUBLANES, g)),
        ],
        out_specs=pl.BlockSpec((ts, 3 * g), lambda bi, si: (bi * nst + si, 0)),
        out_shape=jax.ShapeDtypeStruct((t, 3 * g), BF16),
        scratch_shapes=[pltpu.VMEM((SUBLANES, HALO + ts, g), F32)] + [pltpu.VMEM((HALO + ts, g), F32)] * 2
        + [pltpu.VMEM((ts, g), BF16)],
        compiler_params=_params("parallel", "parallel"),
        name="local_mixers",
    )(proj, proj, conf_dw, ln_g, ln_b, pool_w, pool_scale, sc_dw)


def _post_norm_residual(y, x, gain):
    d = y.shape[1]
    return x + y * _lanes(_inv_rms(y), d) * gain


def _stream_out(t, d, tm, idx):
    specs = [pl.BlockSpec((tm, d), idx), pl.BlockSpec((tm, d), idx), pl.BlockSpec((tm, LANES), idx)]
    shapes = [jax.ShapeDtypeStruct((t, d), F32), jax.ShapeDtypeStruct((t, d), BF16),
              jax.ShapeDtypeStruct((t, LANES), F32)]
    return specs, shapes


def _out_proj_kernel(ya_ref, yl_ref, w_ref, x_ref, g_ref, xo_ref, ho_ref, rso_ref):
    g = GROUP_W

    def project(r0, n):
        return (jnp.dot(ya_ref[r0:r0 + n, :], w_ref[0:g, :], preferred_element_type=F32)
                + jnp.dot(yl_ref[r0:r0 + n, :], w_ref[g:, :], preferred_element_type=F32))

    sizes = [SUB] * (x_ref.shape[0] // SUB)
    starts = [sum(sizes[:k]) for k in range(len(sizes))]
    y_next = project(starts[0], sizes[0])
    for k, (r0, n) in enumerate(zip(starts, sizes)):
        y = y_next
        if k + 1 < len(starts):
            y_next = project(starts[k + 1], sizes[k + 1])
        x_new = _post_norm_residual(y, x_ref[r0:r0 + n, :], g_ref[...])
        _emit_stream(x_new, r0, xo_ref, ho_ref, rso_ref)


def _out_proj(y_att, y_loc, w_out, layer, x, gain):
    t, d = x.shape
    tm = min(TM_OUT, t)
    out_specs, out_shape = _stream_out(t, d, tm, lambda i: (i, 0))
    return pl.pallas_call(
        _out_proj_kernel,
        grid=(t // tm,),
        in_specs=[
            pl.BlockSpec((tm, GROUP_W), lambda i: (i, 0)),
            pl.BlockSpec((tm, 3 * GROUP_W), lambda i: (i, 0)),
            pl.BlockSpec((None, d, d), lambda i: (layer, 0, 0)),
            pl.BlockSpec((tm, d), lambda i: (i, 0)),
            pl.BlockSpec((1, d), lambda i: (0, 0)),
        ],
        out_specs=out_specs,
        out_shape=out_shape,
        compiler_params=_params("parallel"),
        name="out_proj",
    )(y_att, y_loc, w_out, x, gain)


def _mlp_kernel(x_ref, h_ref, rs_ref, w1_ref, w2_ref, g_ref, xo_ref, ho_ref, rso_ref):
    f = pl.program_id(1)
    last = pl.num_programs(1) - 1

    def hidden():
        u = jnp.dot(h_ref[...], w1_ref[...], preferred_element_type=F32)
        return jnp.square(jnp.maximum(u, 0.0)).astype(BF16)

    @pl.when(f == 0)
    def _():
        xo_ref[...] = jnp.dot(hidden(), w2_ref[...], preferred_element_type=F32)

    @pl.when((f > 0) & (f < last))
    def _():
        xo_ref[...] += jnp.dot(hidden(), w2_ref[...], preferred_element_type=F32)

    @pl.when(f == last)
    def _():
        d = xo_ref.shape[1]
        u = hidden()

        def total(r0):
            return xo_ref[r0:r0 + SUB, :] + jnp.dot(u[r0:r0 + SUB, :], w2_ref[...], preferred_element_type=F32)

        starts = list(range(0, x_ref.shape[0], SUB))
        acc_next = total(starts[0])
        for k, r0 in enumerate(starts):
            acc = acc_next
            if k + 1 < len(starts):
                acc_next = total(starts[k + 1])
            r2 = rs_ref[r0:r0 + SUB, :] * rs_ref[r0:r0 + SUB, :]
            ms = jnp.broadcast_to(jnp.mean(acc * acc, axis=-1, keepdims=True), (SUB, LANES))
            scale = r2 * lax.rsqrt(r2 * r2 * ms + EPS)
            x_new = x_ref[r0:r0 + SUB, :] + acc * _lanes(scale, d) * g_ref[...]
            _emit_stream(x_new, r0, xo_ref, ho_ref, rso_ref)


def _mlp(x, h, rs, w1, w2, layer, gain):
    t, d = x.shape
    tf = TF_MLP
    assert w1.shape[2] // tf >= 2, "the MLP kernel needs separate first and last d_ff chunks"
    tm = min(TM_MLP, t)
    out_specs, out_shape = _stream_out(t, d, tm, lambda i, f: (i, 0))
    return pl.pallas_call(
        _mlp_kernel,
        grid=(t // tm, w1.shape[2] // tf),
        in_specs=[
            pl.BlockSpec((tm, d), lambda i, f: (i, 0)),
            pl.BlockSpec((tm, d), lambda i, f: (i, 0)),
            pl.BlockSpec((tm, LANES), lambda i, f: (i, 0)),
            pl.BlockSpec((None, d, tf), lambda i, f: (layer, 0, f)),
            pl.BlockSpec((None, tf, d), lambda i, f: (layer, f, 0)),
            pl.BlockSpec((1, d), lambda i, f: (0, 0)),
        ],
        out_specs=out_specs,
        out_shape=out_shape,
        compiler_params=_params("parallel", "arbitrary"),
        name="mlp",
    )(x, h, rs, w1, w2, gain)


def kernel(x, mix_norm_pre, w_in, b_forget, conf_dw, conf_ln_g, conf_ln_b, pool_w, pool_scale,
           sc_dw, w_out, mix_norm_post, mlp_norm_pre, w_mlp1, w_mlp2, mlp_norm_post):
    b, s, d = x.shape
    depth = w_in.shape[0]
    g = GROUP_W
    o_gate = 3 * g
    o_conf = o_gate + N_HEADS
    col_scale = jnp.where(jnp.arange(w_in.shape[-1]) < g, LOG2E / (HEAD_DIM ** 0.5), 1.0).astype(F32)
    gain_in = mix_norm_pre.astype(F32)[:, :, None]
    w_in_b = (w_in * gain_in * col_scale).astype(BF16)
    w_loc = (w_in[:, :, o_conf:] * gain_in).astype(BF16)
    b_gate = jnp.pad(b_forget.astype(F32), ((0, 0), (0, LANES - N_HEADS)))[:, None, :]
    w_out_b = w_out.astype(BF16)
    w1_b = (w_mlp1 * mlp_norm_pre.astype(F32)[:, :, None]).astype(BF16)
    w2_b = w_mlp2.astype(BF16)
    pool_w_b = pool_w.astype(BF16)
    row = lambda a: a.astype(F32)[:, None, :]
    rep = lambda a: jnp.broadcast_to(a.astype(F32)[..., None, :], a.shape[:-1] + (SUBLANES, a.shape[-1]))
    n_post, m_post = row(mix_norm_post), row(mlp_norm_post)
    ln_g, ln_b, p_scale = rep(conf_ln_g), rep(conf_ln_b), row(pool_scale)
    conf_dw, sc_dw = rep(conf_dw), rep(sc_dw)

    xt = x.reshape(b * s, d)
    for l in range(depth):
        stream = (xt,) if l == 0 else (h, rs)
        loc, qkv, gate, *first = _in_proj(stream, w_loc, w_in_b, l, o_gate, o_gate // LANES)
        if l == 0:
            h, rs = first
        fcol, frow = _gate_prefix(gate, b_gate[l], b, s)
        y_att = _attention(qkv, fcol, frow, b, s)
        y_loc = _local_mixers(loc, conf_dw[l], ln_g[l], ln_b[l], pool_w_b[l], p_scale[l], sc_dw[l], b, s)
        xt, h, rs = _out_proj(y_att, y_loc, w_out_b, l, xt, n_post[l])
        xt, h, rs = _mlp(xt, h, rs, w1_b, w2_b, l, m_post[l])
    return xt.reshape(b, s, d)
```

```python
import jax
import jax.numpy as jnp
from jax import lax
from jax.experimental import pallas as pl
from jax.experimental.pallas import tpu as pltpu

F32 = jnp.float32
BF16 = jnp.bfloat16

EPS = 1e-6
LOG2E = 1.4426950408889634
N_HEADS = 4
HEAD_DIM = 128
GROUP_W = N_HEADS * HEAD_DIM
CONF_K = 31
SC_K = 3
POOL_WINDOWS = (2, 4, 8, 16)
POOL_GROUP = GROUP_W // len(POOL_WINDOWS)
LANES = 128
SUBLANES = 8
HALO = 32
N_LOCAL_COLS = 6 * GROUP_W
VMEM_LIMIT = 56 * 1024 * 1024

TM_IN, TN_IN = 1024, 768
TM_IN_F32 = 512
TM_OUT = 512
TM_MLP, TF_MLP = 512, 1024
SUB = 128
TS_LOCAL = 512
ROWS = 32
TQ = 256


def _params(*sem):
    return pltpu.CompilerParams(dimension_semantics=sem, vmem_limit_bytes=VMEM_LIMIT)


def _lanes(v, width):
    return jnp.concatenate([v] * (width // LANES), axis=1)


def _inv_rms(v):
    ms = jnp.mean(v * v, axis=-1, keepdims=True)
    return jnp.broadcast_to(lax.rsqrt(ms + EPS), (v.shape[0], LANES))


def _emit_stream(x_new, r0, x_ref, h_ref, rs_ref):
    rows = x_new.shape[0]
    x_ref[r0:r0 + rows, :] = x_new
    h_ref[r0:r0 + rows, :] = x_new.astype(BF16)
    rs_ref[r0:r0 + rows, :] = _inv_rms(x_new)


def _in_proj_kernel(h_ref, rs_ref, wl_ref, wq_ref, wg_ref, loc_ref, qkv_ref, gate_ref):
    rs = rs_ref[...]
    for w_ref, o_ref in ((wl_ref, loc_ref), (wq_ref, qkv_ref)):
        for c0 in range(0, o_ref.shape[1], TN_IN):
            acc = jnp.dot(h_ref[...], w_ref[:, c0:c0 + TN_IN], preferred_element_type=F32)
            o_ref[:, c0:c0 + TN_IN] = (acc * _lanes(rs, TN_IN)).astype(BF16)
    gate_ref[...] = jnp.dot(h_ref[...], wg_ref[...], preferred_element_type=F32) * rs


def _in_proj_f32_kernel(x_ref, wl_ref, wq_ref, wg_ref, loc_ref, qkv_ref, gate_ref, h_ref, rs_ref):
    for r0 in range(0, x_ref.shape[0], SUB):
        x = x_ref[r0:r0 + SUB, :]
        h_ref[r0:r0 + SUB, :] = x.astype(BF16)
        rs_ref[r0:r0 + SUB, :] = _inv_rms(x)
    _in_proj_kernel(h_ref, rs_ref, wl_ref, wq_ref, wg_ref, loc_ref, qkv_ref, gate_ref)


def _in_proj(stream, w_loc, w_all, layer, n_qkv, gate_block):
    t, d = stream[0].shape
    first = len(stream) == 1
    tm = min(TM_IN_F32 if first else TM_IN, t)
    n_loc = w_loc.shape[2]
    row_tile = lambda width: pl.BlockSpec((tm, width), lambda i: (i, 0))
    resident = lambda width, blk: pl.BlockSpec((None, d, width), lambda i: (layer, 0, blk),
                                               pipeline_mode=pl.Buffered(1))
    in_specs = [row_tile(d)] + ([] if first else [row_tile(LANES)])
    in_specs += [resident(n_loc, 0), resident(n_qkv, 0), resident(LANES, gate_block)]
    out_specs = [row_tile(n_loc), row_tile(n_qkv), row_tile(LANES)]
    out_shape = [jax.ShapeDtypeStruct((t, n_loc), BF16), jax.ShapeDtypeStruct((t, n_qkv), BF16),
                 jax.ShapeDtypeStruct((t, LANES), F32)]
    if first:
        out_specs += [row_tile(d), row_tile(LANES)]
        out_shape += [jax.ShapeDtypeStruct((t, d), BF16), jax.ShapeDtypeStruct((t, LANES), F32)]
    return pl.pallas_call(
        _in_proj_f32_kernel if first else _in_proj_kernel,
        grid=(t // tm,),
        in_specs=in_specs,
        out_specs=out_specs,
        out_shape=out_shape,
        compiler_params=_params("parallel"),
        name="in_proj",
    )(*stream, w_loc, w_all, w_all)


def _split3(x):
    hi = x.astype(BF16)
    r1 = x - hi.astype(F32)
    mid = r1.astype(BF16)
    lo = (r1 - mid.astype(F32)).astype(BF16)
    return hi, mid, lo


def _gate_kernel(g_ref, b_ref, fcol_ref, frow_ref):
    s = g_ref.shape[0]
    row = lax.broadcasted_iota(jnp.int32, (LANES, LANES), 0)
    col = lax.broadcasted_iota(jnp.int32, (LANES, LANES), 1)
    tri = jnp.where(row >= col, 1.0, 0.0).astype(BF16)
    carry = jnp.zeros((1, LANES), F32)
    for j in range(s // LANES):
        z = g_ref[j * LANES:(j + 1) * LANES, :] + b_ref[...]
        logf = (jnp.minimum(z, 0.0) - jnp.log1p(jnp.exp(-jnp.abs(z)))) * LOG2E
        hi, mid, lo = _split3(logf)
        cum = (jnp.dot(tri, hi, preferred_element_type=F32)
               + jnp.dot(tri, mid, preferred_element_type=F32)
               + jnp.dot(tri, lo, preferred_element_type=F32)) + carry
        carry = cum[LANES - 1:LANES, :]
        fcol_ref[j * LANES:(j + 1) * LANES, :] = cum
        frow_ref[:, j * LANES:(j + 1) * LANES] = cum.T[0:8, :]


def _gate_prefix(gate, bias, b, s):
    return pl.pallas_call(
        _gate_kernel,
        grid=(b,),
        in_specs=[
            pl.BlockSpec((s, LANES), lambda i: (i, 0)),
            pl.BlockSpec((1, LANES), lambda i: (0, 0)),
        ],
        out_specs=[
            pl.BlockSpec((s, LANES), lambda i: (i, 0)),
            pl.BlockSpec((None, 8, s), lambda i: (i, 0, 0)),
        ],
        out_shape=[
            jax.ShapeDtypeStruct((b * s, LANES), F32),
            jax.ShapeDtypeStruct((b, 8, s), F32),
        ],
        compiler_params=_params("parallel"),
        name="gate_prefix",
    )(gate, bias)


def _attn_kernel(q_ref, k_ref, v_ref, fcol_ref, frow_ref, o_ref, s_ref, p_ref):
    s_len = q_ref.shape[0]
    h = pl.program_id(1)
    lane = lax.broadcasted_iota(jnp.int32, (s_len, LANES), 1)
    fq_all = jnp.sum(jnp.where(lane == h, fcol_ref[...], 0.0), axis=1, keepdims=True)
    fk_all = frow_ref[pl.ds(h, 1), :]
    nt = (((1,), (1,)), ((), ()))
    n_q = s_len // TQ

    def score_chunk(i, c, m_part):
        q0, c0 = i * TQ, c * TQ
        sc = lax.dot_general(q_ref[q0:q0 + TQ, :], k_ref[c0:c0 + TQ, :], nt, preferred_element_type=F32)
        sc = sc + (fq_all[q0:q0 + TQ, :] - fk_all[:, c0:c0 + TQ])
        if c == i:
            r_id = lax.broadcasted_iota(jnp.int32, (TQ, TQ), 0)
            c_id = lax.broadcasted_iota(jnp.int32, (TQ, TQ), 1)
            sc = jnp.where(c_id <= r_id, sc, -jnp.inf)
        s_ref[i % 2, :, c0:c0 + TQ] = sc
        for t in range(TQ // LANES):
            m_part = jnp.maximum(m_part, sc[:, t * LANES:(t + 1) * LANES])
        return m_part

    def weight_chunk(i, c, m, l_part):
        c0 = c * TQ
        p = jnp.exp2(s_ref[i % 2, :, c0:c0 + TQ] - m)
        for t in range(TQ // LANES):
            l_part = l_part + p[:, t * LANES:(t + 1) * LANES]
        p_ref[i % 2, :, c0:c0 + TQ] = p.astype(BF16)
        return l_part

    neg_inf = jnp.full((TQ, LANES), -jnp.inf, F32)
    m_cur = jnp.max(score_chunk(0, 0, neg_inf), axis=1, keepdims=True)
    for i in range(n_q):
        l_part = jnp.zeros((TQ, LANES), F32)
        m_part = neg_inf
        for c in range(i + 2):
            if i + 1 < n_q:
                m_part = score_chunk(i + 1, c, m_part)
            if c <= i:
                l_part = weight_chunk(i, c, m_cur, l_part)
        l = jnp.sum(l_part, axis=1, keepdims=True)
        kv = (i + 1) * TQ
        o = jnp.dot(p_ref[i % 2, :, 0:kv], v_ref[0:kv, :], preferred_element_type=F32)
        o_ref[i * TQ:kv, :] = (o / l).astype(BF16)
        m_cur = jnp.max(m_part, axis=1, keepdims=True)


def _attention(qkv, fcol, frow, b, s):
    t = b * s
    return pl.pallas_call(
        _attn_kernel,
        grid=(b, N_HEADS),
        in_specs=[
            pl.BlockSpec((s, HEAD_DIM), lambda i, h: (i, h)),
            pl.BlockSpec((s, HEAD_DIM), lambda i, h: (i, N_HEADS + h)),
            pl.BlockSpec((s, HEAD_DIM), lambda i, h: (i, 2 * N_HEADS + h)),
            pl.BlockSpec((s, LANES), lambda i, h: (i, 0)),
            pl.BlockSpec((None, 8, s), lambda i, h: (i, 0, 0)),
        ],
        out_specs=pl.BlockSpec((s, HEAD_DIM), lambda i, h: (i, h)),
        out_shape=jax.ShapeDtypeStruct((t, GROUP_W), BF16),
        scratch_shapes=[pltpu.VMEM((2, TQ, s), F32), pltpu.VMEM((2, TQ, s), BF16)],
        compiler_params=_params("parallel", "arbitrary"),
        name="fox_attention",
    )(qkv, qkv, qkv, fcol, frow)


def _local_body(i, main_ref, halo_ref, cw_ref, lng_ref, lnb_ref, pw_ref, ps_ref, sw_ref,
                o_ref, u_ref, p_ref, z_ref, pooled_ref):
    ts = main_ref.shape[0]
    g = GROUP_W

    def rows(v):
        return jnp.concatenate([v] * (ROWS // SUBLANES), axis=0)

    def fill(dst0, src):
        src = src.astype(F32)
        a, gt = src[:, 0:g], src[:, g:2 * g]
        u_ref[0, dst0:dst0 + src.shape[0], :] = a * jax.nn.sigmoid(gt)
        p_ref[dst0:dst0 + src.shape[0], :] = src[:, 2 * g:3 * g]
        z_ref[dst0:dst0 + src.shape[0], :] = src[:, 4 * g:5 * g] * src[:, 5 * g:6 * g]

    halo = jnp.where(i > 0, halo_ref[...], jnp.zeros_like(halo_ref))
    fill(0, halo)
    for r in range(0, ts, ROWS):
        fill(HALO + r, main_ref[r:r + ROWS, :])
    n_ext = HALO + ts
    for sft in range(1, SUBLANES):
        for r in range(SUBLANES, n_ext, 4 * ROWS):
            r1 = min(r + 4 * ROWS, n_ext)
            u_ref[sft, r:r1, :] = u_ref[0, r - sft:r1 - sft, :]

    row_id = lax.broadcasted_iota(jnp.int32, (ROWS, POOL_GROUP), 0)
    for r in range(0, ts, ROWS):
        e0 = HALO + r
        acc = jnp.zeros((ROWS, g), F32)
        for k in range(CONF_K):
            back8, sft = divmod(CONF_K - 1 - k, SUBLANES)
            off = e0 - SUBLANES * back8
            acc = acc + rows(cw_ref[k]) * u_ref[sft, off:off + ROWS, :]
        mu = jnp.mean(acc, axis=-1, keepdims=True)
        cen = acc - mu
        var = jnp.mean(cen * cen, axis=-1, keepdims=True)
        y = cen * lax.rsqrt(var + EPS) * rows(lng_ref[...]) + rows(lnb_ref[...])
        o_ref[r:r + ROWS, 0:g] = (y * jax.nn.sigmoid(y)).astype(BF16)

        pos = i * ts + r + row_id
        for gi, w in enumerate(POOL_WINDOWS):
            c0 = gi * POOL_GROUP
            xt = p_ref[e0:e0 + ROWS, c0:c0 + POOL_GROUP]
            win = xt
            for jj in range(1, w):
                win = win + p_ref[e0 - jj:e0 - jj + ROWS, c0:c0 + POOL_GROUP]
            cnt = jnp.minimum(pos + 1, w).astype(F32)
            pooled_ref[r:r + ROWS, c0:c0 + POOL_GROUP] = (win / cnt - xt).astype(BF16)

        conv = jnp.zeros((ROWS, g), F32)
        for k in range(SC_K):
            off = e0 - (SC_K - 1) + k
            conv = conv + rows(sw_ref[k]) * z_ref[off:off + ROWS, :]
        bg = main_ref[r:r + ROWS, 3 * g:4 * g].astype(F32)
        o_ref[r:r + ROWS, 2 * g:3 * g] = (bg * conv).astype(BF16)

    for gi in range(len(POOL_WINDOWS)):
        c0 = gi * POOL_GROUP
        y = jnp.dot(pooled_ref[:, c0:c0 + POOL_GROUP], pw_ref[gi], preferred_element_type=F32)
        o_ref[:, g + c0:g + c0 + POOL_GROUP] = (y * ps_ref[:, c0:c0 + POOL_GROUP]).astype(BF16)


def _local_kernel(*refs):
    _local_body(pl.program_id(1), *refs)


def _local_mixers(proj, conf_dw, ln_g, ln_b, pool_w, pool_scale, sc_dw, b, s):
    t = b * s
    ts = min(TS_LOCAL, s)
    nst = s // ts
    hb = ts // HALO
    g = GROUP_W

    def halo_idx(bi, si):
        return (jnp.maximum(bi * nst * hb + si * hb - 1, 0), 0)

    full = lambda shape: pl.BlockSpec(shape, lambda bi, si: (0,) * len(shape))
    return pl.pallas_call(
        _local_kernel,
        grid=(b, nst),
        in_specs=[
            pl.BlockSpec((ts, N_LOCAL_COLS), lambda bi, si: (bi * nst + si, 0)),
            pl.BlockSpec((HALO, N_LOCAL_COLS), halo_idx),
            full((CONF_K, SUBLANES, g)), full((SUBLANES, g)), full((SUBLANES, g)),
            full((len(POOL_WINDOWS), POOL_GROUP, POOL_GROUP)), full((1, g)), full((SC_K, SUBLANES, g)),
        ],
        out_specs=pl.BlockSpec((ts, 3 * g), lambda bi, si: (bi * nst + si, 0)),
        out_shape=jax.ShapeDtypeStruct((t, 3 * g), BF16),
        scratch_shapes=[pltpu.VMEM((SUBLANES, HALO + ts, g), F32)] + [pltpu.VMEM((HALO + ts, g), F32)] * 2
        + [pltpu.VMEM((ts, g), BF16)],
        compiler_params=_params("parallel", "parallel"),
        name="local_mixers",
    )(proj, proj, conf_dw, ln_g, ln_b, pool_w, pool_scale, sc_dw)


def _post_norm_residual(y, x, gain):
    d = y.shape[1]
    return x + y * _lanes(_inv_rms(y), d) * gain


def _stream_out(t, d, tm, idx):
    specs = [pl.BlockSpec((tm, d), idx), pl.BlockSpec((tm, d), idx), pl.BlockSpec((tm, LANES), idx)]
    shapes = [jax.ShapeDtypeStruct((t, d), F32), jax.ShapeDtypeStruct((t, d), BF16),
              jax.ShapeDtypeStruct((t, LANES), F32)]
    return specs, shapes


def _out_proj_kernel(ya_ref, yl_ref, w_ref, x_ref, g_ref, xo_ref, ho_ref, rso_ref):
    g = GROUP_W

    def project(r0, n):
        return (jnp.dot(ya_ref[r0:r0 + n, :], w_ref[0:g, :], preferred_element_type=F32)
                + jnp.dot(yl_ref[r0:r0 + n, :], w_ref[g:, :], preferred_element_type=F32))

    sizes = [SUB] * (x_ref.shape[0] // SUB)
    starts = [sum(sizes[:k]) for k in range(len(sizes))]
    y_next = project(starts[0], sizes[0])
    for k, (r0, n) in enumerate(zip(starts, sizes)):
        y = y_next
        if k + 1 < len(starts):
            y_next = project(starts[k + 1], sizes[k + 1])
        x_new = _post_norm_residual(y, x_ref[r0:r0 + n, :], g_ref[...])
        _emit_stream(x_new, r0, xo_ref, ho_ref, rso_ref)


def _out_proj(y_att, y_loc, w_out, layer, x, gain):
    t, d = x.shape
    tm = min(TM_OUT, t)
    out_specs, out_shape = _stream_out(t, d, tm, lambda i: (i, 0))
    return pl.pallas_call(
        _out_proj_kernel,
        grid=(t // tm,),
        in_specs=[
            pl.BlockSpec((tm, GROUP_W), lambda i: (i, 0)),
            pl.BlockSpec((tm, 3 * GROUP_W), lambda i: (i, 0)),
            pl.BlockSpec((None, d, d), lambda i: (layer, 0, 0)),
            pl.BlockSpec((tm, d), lambda i: (i, 0)),
            pl.BlockSpec((1, d), lambda i: (0, 0)),
        ],
        out_specs=out_specs,
        out_shape=out_shape,
        compiler_params=_params("parallel"),
        name="out_proj",
    )(y_att, y_loc, w_out, x, gain)


def _mlp_kernel(x_ref, h_ref, rs_ref, w1_ref, w2_ref, g_ref, xo_ref, ho_ref, rso_ref):
    f = pl.program_id(1)
    last = pl.num_programs(1) - 1

    def hidden():
        u = jnp.dot(h_ref[...], w1_ref[...], preferred_element_type=F32)
        return jnp.square(jnp.maximum(u, 0.0)).astype(BF16)

    def chunk_out():
        half = w1_ref.shape[1] // 2
        y = None
        for c0 in (0, half):
            u = jnp.dot(h_ref[...], w1_ref[:, c0:c0 + half], preferred_element_type=F32)
            u = jnp.square(jnp.maximum(u, 0.0)).astype(BF16)
            part = jnp.dot(u, w2_ref[c0:c0 + half, :], preferred_element_type=F32)
            y = part if y is None else y + part
        return y

    @pl.when(f == 0)
    def _():
        xo_ref[...] = chunk_out()

    @pl.when((f > 0) & (f < last))
    def _():
        xo_ref[...] += chunk_out()

    @pl.when(f == last)
    def _():
        d = xo_ref.shape[1]
        u = hidden()

        def total(r0):
            return xo_ref[r0:r0 + SUB, :] + jnp.dot(u[r0:r0 + SUB, :], w2_ref[...], preferred_element_type=F32)

        starts = list(range(0, x_ref.shape[0], SUB))
        acc_next = total(starts[0])
        for k, r0 in enumerate(starts):
            acc = acc_next
            if k + 1 < len(starts):
                acc_next = total(starts[k + 1])
            r2 = rs_ref[r0:r0 + SUB, :] * rs_ref[r0:r0 + SUB, :]
            ms = jnp.broadcast_to(jnp.mean(acc * acc, axis=-1, keepdims=True), (SUB, LANES))
            scale = r2 * lax.rsqrt(r2 * r2 * ms + EPS)
            x_new = x_ref[r0:r0 + SUB, :] + acc * _lanes(scale, d) * g_ref[...]
            _emit_stream(x_new, r0, xo_ref, ho_ref, rso_ref)


def _mlp(x, h, rs, w1, w2, layer, gain):
    t, d = x.shape
    tf = TF_MLP
    assert w1.shape[2] // tf >= 2, "the MLP kernel needs separate first and last d_ff chunks"
    tm = min(TM_MLP, t)
    out_specs, out_shape = _stream_out(t, d, tm, lambda i, f: (i, 0))
    return pl.pallas_call(
        _mlp_kernel,
        grid=(t // tm, w1.shape[2] // tf),
        in_specs=[
            pl.BlockSpec((tm, d), lambda i, f: (i, 0)),
            pl.BlockSpec((tm, d), lambda i, f: (i, 0)),
            pl.BlockSpec((tm, LANES), lambda i, f: (i, 0)),
            pl.BlockSpec((None, d, tf), lambda i, f: (layer, 0, f)),
            pl.BlockSpec((None, tf, d), lambda i, f: (layer, f, 0)),
            pl.BlockSpec((1, d), lambda i, f: (0, 0)),
        ],
        out_specs=out_specs,
        out_shape=out_shape,
        compiler_params=_params("parallel", "arbitrary"),
        name="mlp",
    )(x, h, rs, w1, w2, gain)


def kernel(x, mix_norm_pre, w_in, b_forget, conf_dw, conf_ln_g, conf_ln_b, pool_w, pool_scale,
           sc_dw, w_out, mix_norm_post, mlp_norm_pre, w_mlp1, w_mlp2, mlp_norm_post):
    b, s, d = x.shape
    depth = w_in.shape[0]
    g = GROUP_W
    o_gate = 3 * g
    o_conf = o_gate + N_HEADS
    col_scale = jnp.where(jnp.arange(w_in.shape[-1]) < g, LOG2E / (HEAD_DIM ** 0.5), 1.0).astype(F32)
    w_in_b = (w_in * mix_norm_pre.astype(F32)[:, :, None] * col_scale).astype(BF16)
    w_loc = w_in_b[:, :, o_conf:]
    b_gate = jnp.pad(b_forget.astype(F32), ((0, 0), (0, LANES - N_HEADS)))[:, None, :]
    w_out_b = w_out.astype(BF16)
    w1_b = (w_mlp1 * mlp_norm_pre.astype(F32)[:, :, None]).astype(BF16)
    w2_b = w_mlp2.astype(BF16)
    pool_w_b = pool_w.astype(BF16)
    row = lambda a: a.astype(F32)[:, None, :]
    rep = lambda a: jnp.broadcast_to(a.astype(F32)[..., None, :], a.shape[:-1] + (SUBLANES, a.shape[-1]))
    n_post, m_post = row(mix_norm_post), row(mlp_norm_post)
    ln_g, ln_b, p_scale = rep(conf_ln_g), rep(conf_ln_b), row(pool_scale)
    conf_dw, sc_dw = rep(conf_dw), rep(sc_dw)

    xt = x.reshape(b * s, d)
    for l in range(depth):
        stream = (xt,) if l == 0 else (h, rs)
        loc, qkv, gate, *first = _in_proj(stream, w_loc, w_in_b, l, o_gate, o_gate // LANES)
        if l == 0:
            h, rs = first
        fcol, frow = _gate_prefix(gate, b_gate[l], b, s)
        y_att = _attention(qkv, fcol, frow, b, s)
        y_loc = _local_mixers(loc, conf_dw[l], ln_g[l], ln_b[l], pool_w_b[l], p_scale[l], sc_dw[l], b, s)
        xt, h, rs = _out_proj(y_att, y_loc, w_out_b, l, xt, n_post[l])
        xt, h, rs = _mlp(xt, h, rs, w1_b, w2_b, l, m_post[l])
    return xt.reshape(b, s, d)
```
